```python
import math
import jax
import jax.numpy as jnp
from jax import lax
import numpy as np


D_MODEL = 2048
BATCH = 1
SEQ = 8192
DEPTH = 4

N_MIXERS = 2
N_ATTN_LAYERS = (DEPTH + 1) // 2
N_SSD_LAYERS = DEPTH // 2
EPS = 1e-6
NEG_INF = -1e30

D_FF = 5632

DA_HEADS = 8
DA_HEAD_DIM = D_MODEL // (2 * DA_HEADS)
DA_V_DIM = 2 * DA_HEAD_DIM
Q_BLOCK = 128

REL_BUCKETS = 32
REL_MAX_DIST = 128

SSD_EXPAND = 2
SSD_INNER = SSD_EXPAND * D_MODEL
SSD_HEAD_DIM = 64
SSD_HEADS = SSD_INNER // SSD_HEAD_DIM
SSD_GROUPS = 8
SSD_STATE = 128
SSD_CONV = 4
SSD_CHUNK = 128
SSD_CONV_DIM = SSD_INNER + 2 * SSD_GROUPS * SSD_STATE
SSD_IN_DIM = 2 * SSD_INNER + 2 * SSD_GROUPS * SSD_STATE + SSD_HEADS

MEM_LEN = 256
XA_HEADS = 4
XA_HEAD_DIM = 128

kernel_name = 'hybrid_diffattn_mamba2_macaron_memxattn'


def rms_norm(x, g):
    xf = x.astype(jnp.float32)
    y = xf * lax.rsqrt(jnp.mean(xf * xf, axis=-1, keepdims=True) + EPS)
    return (y * g.astype(jnp.float32)).astype(x.dtype)


def swiglu_ffn(h, w_gate, w_up, w_down):
    return (jax.nn.silu(h @ w_gate) * (h @ w_up)) @ w_down


def t5_causal_bucket(dist):
    n = jnp.maximum(dist, 0)
    max_exact = REL_BUCKETS // 2
    nf = jnp.maximum(n, 1).astype(jnp.float32)
    large = max_exact + (jnp.log(nf / max_exact) / math.log(REL_MAX_DIST / max_exact)
                         * (REL_BUCKETS - max_exact)).astype(jnp.int32)
    large = jnp.minimum(large, REL_BUCKETS - 1)
    return jnp.where(n < max_exact, n, large)


def diff_attention(h, w_qkv, w_o, q_gain, k_gain, lam_vecs, subln_gain, rel_bias, lambda_init):
    b, s, _ = h.shape
    q, k, v = jnp.split(h @ w_qkv, 3, axis=-1)
    q = q.reshape(b, s, DA_HEADS, 2, DA_HEAD_DIM)
    k = k.reshape(b, s, DA_HEADS, 2, DA_HEAD_DIM)
    v = v.reshape(b, s, DA_HEADS, DA_V_DIM)
    q = rms_norm(q, q_gain) * (DA_HEAD_DIM ** -0.5)
    k = rms_norm(k, k_gain)
    lv = lam_vecs.astype(jnp.float32)
    lam = jnp.exp(jnp.sum(lv[0] * lv[1])) - jnp.exp(jnp.sum(lv[2] * lv[3])) + lambda_init
    nb = s // Q_BLOCK
    qb = q.reshape(b, nb, Q_BLOCK, DA_HEADS, 2, DA_HEAD_DIM).transpose(1, 0, 3, 4, 2, 5)
    kt = k.transpose(0, 2, 3, 1, 4)
    vt = v.transpose(0, 2, 1, 3)
    k_pos = jnp.arange(s, dtype=jnp.int32)
    table = rel_bias.astype(jnp.float32)

    def block(args):
        q_blk, i = args
        q_pos = i * Q_BLOCK + jnp.arange(Q_BLOCK, dtype=jnp.int32)
        dist = q_pos[:, None] - k_pos[None, :]
        bias = jnp.transpose(table[t5_causal_bucket(dist)], (2, 0, 1))
        logits = jnp.einsum('bhmqd,bhmkd->bhmqk', q_blk, kt).astype(jnp.float32) + bias[None, :, None]
        logits = jnp.where((dist >= 0)[None, None, None], logits, NEG_INF)
        p = jax.nn.softmax(logits, axis=-1)
        a = p[:, :, 0] - lam * p[:, :, 1]
        return jnp.einsum('bhqk,bhkv->bhqv', a.astype(vt.dtype), vt)

    o = lax.map(block, (qb, jnp.arange(nb, dtype=jnp.int32)))
    o = o.transpose(1, 0, 3, 2, 4).reshape(b, s, DA_HEADS, DA_V_DIM)
    o = rms_norm(o, subln_gain) * (1.0 - lambda_init)
    return o.reshape(b, s, DA_HEADS * DA_V_DIM) @ w_o


def causal_depthwise_conv(x, w, bias):
    out = lax.conv_general_dilated(
        x, w[:, None, :].astype(x.dtype), window_strides=(1,), padding=[(SSD_CONV - 1, 0)],
        dimension_numbers=('NWC', 'WIO', 'NWC'), feature_group_count=x.shape[-1])
    return out + bias


def ssd_chunked_scan(x, dt, a, bm, cm):
    b, s = x.shape[:2]
    nc = s // SSD_CHUNK
    r = SSD_HEADS // SSD_GROUPS

    def to_chunks(t):
        return jnp.moveaxis(t.reshape(b, nc, SSD_CHUNK, *t.shape[2:]), 1, 0)

    xc = to_chunks((x.astype(jnp.float32) * dt[..., None]).reshape(b, s, SSD_GROUPS, r, SSD_HEAD_DIM))
    ac = to_chunks((dt * a).reshape(b, s, SSD_GROUPS, r))
    bc = to_chunks(bm.astype(jnp.float32))
    cc = to_chunks(cm.astype(jnp.float32))
    idx = jnp.arange(SSD_CHUNK)
    causal = (idx[:, None] >= idx[None, :])[None, :, :, None, None]

    def step(state, inp):
        x_c, a_c, b_c, c_c = inp
        a_cum = jnp.cumsum(a_c, axis=1)
        seg = a_cum[:, :, None] - a_cum[:, None, :]
        decay = jnp.exp(jnp.where(causal, seg, -jnp.inf))
        cb = jnp.einsum('blgn,bsgn->blsg', c_c, b_c)
        y = jnp.einsum('blsg,blsgr,bsgrp->blgrp', cb, decay, x_c)
        y = y + jnp.einsum('blgn,bgrpn->blgrp', c_c, state) * jnp.exp(a_cum)[..., None]
        a_last = a_cum[:, -1]
        w = jnp.exp(a_last[:, None] - a_cum)
        state = state * jnp.exp(a_last)[..., None, None] + jnp.einsum('bsgn,bsgr,bsgrp->bgrpn', b_c, w, x_c)
        return state, y

    state0 = jnp.zeros((b, SSD_GROUPS, r, SSD_HEAD_DIM, SSD_STATE), jnp.float32)
    _, y = lax.scan(step, state0, (xc, ac, bc, cc))
    return jnp.moveaxis(y, 0, 1).reshape(b, s, SSD_HEADS, SSD_HEAD_DIM)


def ssd_mixer(h, w_in, conv_w, conv_b, dt_bias, a_log, d_skip, norm_g, w_out):
    b, s, _ = h.shape
    z, xbc, dt = jnp.split(h @ w_in, [SSD_INNER, SSD_INNER + SSD_CONV_DIM], axis=-1)
    xbc = jax.nn.silu(causal_depthwise_conv(xbc, conv_w, conv_b))
    xs, bm, cm = jnp.split(xbc, [SSD_INNER, SSD_INNER + SSD_GROUPS * SSD_STATE], axis=-1)
    xs = xs.reshape(b, s, SSD_HEADS, SSD_HEAD_DIM)
    bm = bm.reshape(b, s, SSD_GROUPS, SSD_STATE)
    cm = cm.reshape(b, s, SSD_GROUPS, SSD_STATE)
    dt = jax.nn.softplus(dt.astype(jnp.float32) + dt_bias.astype(jnp.float32))
    a = -jnp.exp(a_log.astype(jnp.float32))
    y = ssd_chunked_scan(xs, dt, a, bm, cm) + xs.astype(jnp.float32) * d_skip.astype(jnp.float32)[:, None]
    y = y.reshape(b, s, SSD_INNER).astype(h.dtype)
    y = rms_norm(y * jax.nn.silu(z), norm_g)
    return y @ w_out


def memory_cross_attention(h, mem_h, w_q, w_kv, w_o, q_gain, k_gain):
    b, s, _ = h.shape
    m = mem_h.shape[1]
    q = (h @ w_q).reshape(b, s, XA_HEADS, XA_HEAD_DIM)
    k, v = jnp.split(mem_h @ w_kv, 2, axis=-1)
    k = k.reshape(b, m, XA_HEADS, XA_HEAD_DIM)
    v = v.reshape(b, m, XA_HEADS, XA_HEAD_DIM)
    q = rms_norm(q, q_gain) * (XA_HEAD_DIM ** -0.5)
    k = rms_norm(k, k_gain)
    logits = jnp.einsum('bshd,bmhd->bhsm', q, k).astype(jnp.float32)
    p = jax.nn.softmax(logits, axis=-1).astype(v.dtype)
    o = jnp.einsum('bhsm,bmhd->bshd', p, v).reshape(b, s, XA_HEADS * XA_HEAD_DIM)
    return o @ w_o


def setup_inputs(seed: int = 0):
    key = jax.random.key(seed)
    ks = iter(list(jax.random.split(key, 40)))
    f32 = jnp.float32

    def normal(shape, scale):
        return jax.random.normal(next(ks), shape, f32) * scale

    def gain(shape):
        return 1.0 + normal(shape, 0.02)

    D, F, L = D_MODEL, D_FF, DEPTH
    NA, NS = N_ATTN_LAYERS, N_SSD_LAYERS
    XW = XA_HEADS * XA_HEAD_DIM
    inp = {}
    inp['x'] = normal((BATCH, SEQ, D), 1.0)
    inp['mem'] = normal((BATCH, MEM_LEN, D), 1.0)
    inp['rel_bias'] = normal((REL_BUCKETS, DA_HEADS), 0.5)
    inp['ffn1_norm'] = gain((L, D))
    inp['ffn1_w_gate'] = normal((L, D, F), D ** -0.5)
    inp['ffn1_w_up'] = normal((L, D, F), D ** -0.5)
    inp['ffn1_w_down'] = normal((L, F, D), F ** -0.5)
    inp['mix_norm'] = gain((L, D))
    inp['attn_w_qkv'] = normal((NA, D, 3 * D), D ** -0.5)
    inp['attn_w_o'] = normal((NA, DA_HEADS * DA_V_DIM, D), (DA_HEADS * DA_V_DIM) ** -0.5)
    inp['attn_q_norm'] = gain((NA, DA_HEAD_DIM))
    inp['attn_k_norm'] = gain((NA, DA_HEAD_DIM))
    inp['attn_lambda'] = normal((NA, 4, DA_HEAD_DIM), 0.1)
    inp['attn_subln'] = gain((NA, DA_V_DIM))
    inp['ssd_w_in'] = normal((NS, D, SSD_IN_DIM), D ** -0.5)
    inp['ssd_conv_w'] = jax.random.uniform(next(ks), (NS, SSD_CONV, SSD_CONV_DIM), f32, -0.5, 0.5)
    inp['ssd_conv_b'] = normal((NS, SSD_CONV_DIM), 0.02)
    dt = jnp.exp(jax.random.uniform(next(ks), (NS, SSD_HEADS), f32, math.log(1e-3), math.log(1e-1)))
    inp['ssd_dt_bias'] = dt + jnp.log(-jnp.expm1(-dt))
    inp['ssd_a_log'] = jnp.log(jax.random.uniform(next(ks), (NS, SSD_HEADS), f32, 1.0, 16.0))
    inp['ssd_d'] = gain((NS, SSD_HEADS))
    inp['ssd_norm'] = gain((NS, SSD_INNER))
    inp['ssd_w_out'] = normal((NS, SSD_INNER, D), SSD_INNER ** -0.5)
    inp['xattn_norm'] = gain((L, D))
    inp['mem_norm'] = gain((L, D))
    inp['xattn_w_q'] = normal((L, D, XW), D ** -0.5)
    inp['xattn_w_kv'] = normal((L, D, 2 * XW), D ** -0.5)
    inp['xattn_w_o'] = normal((L, XW, D), XW ** -0.5)
    inp['xattn_q_norm'] = gain((L, XA_HEAD_DIM))
    inp['xattn_k_norm'] = gain((L, XA_HEAD_DIM))
    inp['ffn2_norm'] = gain((L, D))
    inp['ffn2_w_gate'] = normal((L, D, F), D ** -0.5)
    inp['ffn2_w_up'] = normal((L, D, F), D ** -0.5)
    inp['ffn2_w_down'] = normal((L, F, D), F ** -0.5)
    return inp


def reference(x, mem, rel_bias,
              ffn1_norm, ffn1_w_gate, ffn1_w_up, ffn1_w_down,
              mix_norm,
              attn_w_qkv, attn_w_o, attn_q_norm, attn_k_norm, attn_lambda, attn_subln,
              ssd_w_in, ssd_conv_w, ssd_conv_b, ssd_dt_bias, ssd_a_log, ssd_d, ssd_norm, ssd_w_out,
              xattn_norm, mem_norm, xattn_w_q, xattn_w_kv, xattn_w_o, xattn_q_norm, xattn_k_norm,
              ffn2_norm, ffn2_w_gate, ffn2_w_up, ffn2_w_down):
    for i in range(DEPTH):
        x = x + 0.5 * swiglu_ffn(rms_norm(x, ffn1_norm[i]), ffn1_w_gate[i], ffn1_w_up[i], ffn1_w_down[i])
        h = rms_norm(x, mix_norm[i])
        j = i // N_MIXERS
        if i % N_MIXERS == 0:
            lambda_init = 0.8 - 0.6 * math.exp(-0.3 * i)
            x = x + diff_attention(h, attn_w_qkv[j], attn_w_o[j], attn_q_norm[j], attn_k_norm[j],
                                   attn_lambda[j], attn_subln[j], rel_bias, lambda_init)
        else:
            x = x + ssd_mixer(h, ssd_w_in[j], ssd_conv_w[j], ssd_conv_b[j], ssd_dt_bias[j],
                              ssd_a_log[j], ssd_d[j], ssd_norm[j], ssd_w_out[j])
        x = x + memory_cross_attention(rms_norm(x, xattn_norm[i]), rms_norm(mem, mem_norm[i]),
                                       xattn_w_q[i], xattn_w_kv[i], xattn_w_o[i],
                                       xattn_q_norm[i], xattn_k_norm[i])
        x = x + 0.5 * swiglu_ffn(rms_norm(x, ffn2_norm[i]), ffn2_w_gate[i], ffn2_w_up[i], ffn2_w_down[i])
    return x
```

```python
import functools
import math

import jax
import jax.numpy as jnp
from jax import lax
from jax.experimental import pallas as pl
from jax.experimental.pallas import tpu as pltpu

F32 = jnp.float32
BF16 = jnp.bfloat16

EPS = 1e-6
NEG_INF = -1e30

DEPTH = 4
N_MIXERS = 2

DA_HEADS = 8
DA_HEAD_DIM = 128
DA_V_DIM = 2 * DA_HEAD_DIM
REL_BUCKETS = 32
REL_MAX_DIST = 128
ATTN_BLOCK = 512

SSD_HEAD_DIM = 64
SSD_HEADS = 64
SSD_GROUPS = 8
SSD_STATE = 128
SSD_CONV = 4
SSD_CHUNK = 128
SSD_INNER = SSD_HEADS * SSD_HEAD_DIM
SSD_HEADS_PER_GROUP = SSD_HEADS // SSD_GROUPS
SSD_GROUP_WIDTH = SSD_HEADS_PER_GROUP * SSD_HEAD_DIM
SSD_COL_BLOCK = 2048
CONV_HALO = 8

XA_HEADS = 4
XA_HEAD_DIM = 128

LANES = 128
VMEM_LIMIT_BYTES = 56 * 1024 * 1024


def _params(*sem):
    return pltpu.CompilerParams(dimension_semantics=sem, vmem_limit_bytes=VMEM_LIMIT_BYTES)


def _rms(x, g):
    return x * lax.rsqrt(jnp.mean(x * x, axis=-1, keepdims=True) + EPS) * g


def _silu(x):
    return x * jax.nn.sigmoid(x)


def _dot(a, b):
    return jnp.dot(a, b, preferred_element_type=F32)


def _dot_nt(a, b):
    return lax.dot_general(a, b, (((1,), (1,)), ((), ())), preferred_element_type=F32)


def _dot_tn(a, b):
    return lax.dot_general(a, b, (((0,), (0,)), ((), ())), preferred_element_type=F32)


def _ffn_kernel(x_ref, g_ref, wg_ref, wu_ref, wd_ref, o_ref, h_ref):
    j = pl.program_id(1)

    @pl.when(j == 0)
    def _():
        h_ref[...] = _rms(x_ref[...], g_ref[...]).astype(BF16)
        o_ref[...] = jnp.zeros_like(o_ref)

    h = h_ref[...]
    gate = _dot(h, wg_ref[...])
    up = _dot(h, wu_ref[...])
    a = (_silu(gate) * up).astype(BF16)
    o_ref[...] += _dot(a, wd_ref[...])

    @pl.when(j == pl.num_programs(1) - 1)
    def _():
        o_ref[...] = x_ref[...] + 0.5 * o_ref[...]


def _ffn(x, g, wg, wu, wd, *, tm=512, tf=512):
    s, d = x.shape
    f = wg.shape[1]
    tm = min(tm, s)
    return pl.pallas_call(
        _ffn_kernel,
        grid=(s // tm, f // tf),
        in_specs=[
            pl.BlockSpec((tm, d), lambda i, j: (i, 0)),
            pl.BlockSpec((1, d), lambda i, j: (0, 0)),
            pl.BlockSpec((d, tf), lambda i, j: (0, j)),
            pl.BlockSpec((d, tf), lambda i, j: (0, j)),
            pl.BlockSpec((tf, d), lambda i, j: (j, 0)),
        ],
        out_specs=pl.BlockSpec((tm, d), lambda i, j: (i, 0)),
        out_shape=jax.ShapeDtypeStruct((s, d), F32),
        scratch_shapes=[pltpu.VMEM((tm, d), BF16)],
        compiler_params=_params("parallel", "arbitrary"),
        name="ffn",
    )(x, g.reshape(1, d), wg, wu, wd)


def _proj_kernel(x_ref, g_ref, w_ref, cg_ref, o_ref, h_ref, *, n_norm_tiles):
    j = pl.program_id(1)

    @pl.when(j == 0)
    def _():
        h_ref[...] = _rms(x_ref[...], g_ref[...]).astype(BF16)

    y = _dot(h_ref[...], w_ref[...])

    if n_norm_tiles:
        @pl.when(j < n_norm_tiles)
        def _():
            for c in range(y.shape[1] // LANES):
                cols = slice(c * LANES, (c + 1) * LANES)
                o_ref[:, cols] = _rms(y[:, cols], cg_ref[:, cols]).astype(o_ref.dtype)

        @pl.when(j >= n_norm_tiles)
        def _():
            o_ref[...] = y.astype(o_ref.dtype)
    else:
        o_ref[...] = y.astype(o_ref.dtype)


def _proj(x, g, w, col_gain, *, n_cols, n_norm_cols, out_dtype, tm=512, tn=512):
    s, d = x.shape
    tm = min(tm, s)
    tn = min(tn, n_cols)
    assert n_cols % tn == 0 and n_norm_cols % tn == 0
    return pl.pallas_call(
        functools.partial(_proj_kernel, n_norm_tiles=n_norm_cols // tn),
        grid=(s // tm, n_cols // tn),
        in_specs=[
            pl.BlockSpec((tm, d), lambda i, j: (i, 0)),
            pl.BlockSpec((1, d), lambda i, j: (0, 0)),
            pl.BlockSpec((d, tn), lambda i, j: (0, j)),
            pl.BlockSpec((1, tn), lambda i, j: (0, j)),
        ],
        out_specs=pl.BlockSpec((tm, tn), lambda i, j: (i, j)),
        out_shape=jax.ShapeDtypeStruct((s, n_cols), out_dtype),
        scratch_shapes=[pltpu.VMEM((tm, d), BF16)],
        compiler_params=_params("parallel", "arbitrary"),
        name="proj",
    )(x, g.reshape(1, d), w, col_gain)


def _out_proj_kernel(a_ref, w_ref, x_ref, o_ref):
    o_ref[...] = x_ref[...] + _dot(a_ref[...], w_ref[...])


def _out_proj(a, w, x, *, tm=512, tn=512):
    s, k = a.shape
    d = w.shape[1]
    tm = min(tm, s)
    return pl.pallas_call(
        _out_proj_kernel,
        grid=(s // tm, d // tn),
        in_specs=[
            pl.BlockSpec((tm, k), lambda i, j: (i, 0)),
            pl.BlockSpec((k, tn), lambda i, j: (0, j)),
            pl.BlockSpec((tm, tn), lambda i, j: (i, j)),
        ],
        out_specs=pl.BlockSpec((tm, tn), lambda i, j: (i, j)),
        out_shape=jax.ShapeDtypeStruct((s, d), F32),
        compiler_params=_params("parallel", "arbitrary"),
        name="out_proj",
    )(a, w, x)


def _bias_kernel(tab_ref, o_ref, *, t):
    h = pl.program_id(0)
    row = lax.broadcasted_iota(jnp.int32, (t, t), 0)
    col = lax.broadcasted_iota(jnp.int32, (t, t), 1)
    max_exact = REL_BUCKETS // 2
    far = tab_ref[REL_BUCKETS - 1, h]
    for blk in range(2):
        dist = row - col + blk * t
        n = jnp.maximum(dist, 0)
        nf = jnp.maximum(n, 1).astype(F32)
        large = max_exact + (jnp.log(nf / max_exact) / math.log(REL_MAX_DIST / max_exact)
                             * (REL_BUCKETS - max_exact)).astype(jnp.int32)
        large = jnp.minimum(large, REL_BUCKETS - 1)
        bucket = jnp.where(n < max_exact, n, large)
        bias = jnp.zeros((t, t), F32)
        for b in range(REL_BUCKETS - 1):
            bias = jnp.where(bucket == b, tab_ref[b, h] - far, bias)
        if blk == 0:
            bias = jnp.where(dist >= 0, bias, NEG_INF)
        o_ref[0, blk] = bias


def _bias_tiles(rel_bias, t):
    assert t >= REL_MAX_DIST
    return pl.pallas_call(
        functools.partial(_bias_kernel, t=t),
        grid=(DA_HEADS,),
        in_specs=[pl.BlockSpec(memory_space=pltpu.SMEM)],
        out_specs=pl.BlockSpec((1, 2, t, t), lambda h: (h, 0, 0, 0)),
        out_shape=jax.ShapeDtypeStruct((DA_HEADS, 2, t, t), F32),
        compiler_params=_params("arbitrary"),
        name="rel_bias_tiles",
    )(rel_bias)


def _attn_kernel(q_ref, k_ref, v_ref, bias_ref, lv_ref, sg_ref, o_ref, m_ref, l_ref, acc_ref, *, t, lambda_init):
    qb = pl.program_id(1)
    dh = DA_HEAD_DIM
    m_ref[...] = jnp.full(m_ref.shape, NEG_INF, F32)
    l_ref[...] = jnp.zeros(l_ref.shape, F32)
    acc_ref[...] = jnp.zeros(acc_ref.shape, F32)

    def block(start, bias):
        v = v_ref[pl.ds(start, t), :]
        for mp in range(2):
            cols = slice(mp * dh, (mp + 1) * dh)
            s = _dot_nt(q_ref[:, cols], k_ref[pl.ds(start, t), cols])
            if bias is not None:
                s = s + bias
            m_prev = m_ref[mp]
            m_new = jnp.maximum(m_prev, jnp.max(s, axis=-1, keepdims=True))
            alpha = jnp.exp(m_prev - m_new)
            p = jnp.exp(s - m_new)
            l_ref[mp] = alpha * l_ref[mp] + jnp.sum(p, axis=-1, keepdims=True)
            acc_ref[mp] = alpha * acc_ref[mp] + _dot(p.astype(BF16), v)
            m_ref[mp] = m_new

    def far_block(kb, carry):
        block(pl.multiple_of(kb * t, t), None)
        return carry

    lax.fori_loop(0, jnp.maximum(qb - 1, 0), far_block, 0)

    @pl.when(qb > 0)
    def _():
        block(pl.multiple_of((qb - 1) * t, t), bias_ref[0, 1])

    block(pl.multiple_of(qb * t, t), bias_ref[0, 0])

    lv = lv_ref[...]
    lam = (jnp.exp(jnp.sum(lv[0:1] * lv[1:2], axis=-1, keepdims=True))
           - jnp.exp(jnp.sum(lv[2:3] * lv[3:4], axis=-1, keepdims=True)) + lambda_init)
    o = acc_ref[0] / l_ref[0] - lam * (acc_ref[1] / l_ref[1])
    o_ref[...] = (_rms(o, sg_ref[...]) * (1.0 - lambda_init)).astype(o_ref.dtype)


def _diff_attention(qkv, bias, lam_vecs, subln_gain, lambda_init, *, t):
    s = qkv.shape[0]
    t = min(t, s)
    hw = DA_V_DIM
    return pl.pallas_call(
        functools.partial(_attn_kernel, t=t, lambda_init=lambda_init),
        grid=(DA_HEADS, s // t),
        in_specs=[
            pl.BlockSpec((t, hw), lambda h, i: (i, h)),
            pl.BlockSpec((s, hw), lambda h, i: (0, DA_HEADS + h)),
            pl.BlockSpec((s, hw), lambda h, i: (0, 2 * DA_HEADS + h)),
            pl.BlockSpec((1, 2, t, t), lambda h, i: (h, 0, 0, 0)),
            pl.BlockSpec((4, DA_HEAD_DIM), lambda h, i: (0, 0)),
            pl.BlockSpec((1, hw), lambda h, i: (0, 0)),
        ],
        out_specs=pl.BlockSpec((t, hw), lambda h, i: (i, h)),
        out_shape=jax.ShapeDtypeStruct((s, DA_HEADS * hw), BF16),
        scratch_shapes=[
            pltpu.VMEM((2, t, 1), F32),
            pltpu.VMEM((2, t, 1), F32),
            pltpu.VMEM((2, t, hw), F32),
        ],
        compiler_params=_params("parallel", "arbitrary"),
        name="diff_attn",
    )(qkv, qkv, qkv, bias, lam_vecs, subln_gain.reshape(1, hw))


def _ssd_kernel(z0_ref, z1_ref, x0_ref, x1_ref, bc_ref, dt_ref, cw_ref, cb_ref, dtb_ref, alog_ref, dskip_ref,
                ng_ref, o_ref, ext_ref, xs_ref, bcs_ref, y_ref, state_ref):
    c = pl.program_id(0)
    ln = SSD_CHUNK
    cbw = SSD_COL_BLOCK
    ns = SSD_STATE
    gw = SSD_GROUP_WIDTH

    @pl.when(c == 0)
    def _():
        ext_ref[:, 0:CONV_HALO, :] = jnp.zeros((3, CONV_HALO, cbw), F32)
        state_ref[...] = jnp.zeros(state_ref.shape, F32)

    for blk, src in enumerate((x0_ref, x1_ref, bc_ref)):
        cols = slice(blk * cbw, (blk + 1) * cbw)
        ext_ref[blk, CONV_HALO:CONV_HALO + ln, :] = src[...]
        acc = cb_ref[:, cols] + cw_ref[0:1, cols] * ext_ref[blk, pl.ds(CONV_HALO - SSD_CONV + 1, ln), :]
        for k in range(1, SSD_CONV):
            acc = acc + cw_ref[k:k + 1, cols] * ext_ref[blk, pl.ds(CONV_HALO - SSD_CONV + 1 + k, ln), :]
        ext_ref[blk, 0:CONV_HALO, :] = ext_ref[blk, ln:ln + CONV_HALO, :]
        act = _silu(acc)
        if blk < 2:
            xs_ref[:, cols] = act
        else:
            bcs_ref[...] = act.astype(BF16)

    dt = jax.nn.softplus(dt_ref[:, 0:SSD_HEADS] + dtb_ref[...])
    a = dt * (-jnp.exp(alog_ref[...]))
    row = lax.broadcasted_iota(jnp.int32, (ln, ln), 0)
    col = lax.broadcasted_iota(jnp.int32, (ln, ln), 1)
    causal = row >= col
    acum = jnp.dot(causal.astype(F32), a, preferred_element_type=F32, precision=lax.Precision.HIGHEST)
    acum_t = acum.T
    lane_lo = lax.broadcasted_iota(jnp.int32, (ln, LANES), 1) < SSD_HEAD_DIM

    for g in range(SSD_GROUPS):
        b_g = bcs_ref[:, g * ns:(g + 1) * ns]
        c_g = bcs_ref[:, SSD_GROUPS * ns + g * ns:SSD_GROUPS * ns + (g + 1) * ns]
        cb = _dot_nt(c_g, b_g)
        y_state = _dot(c_g, state_ref[g].astype(BF16))
        xw_parts = []
        decay_parts = []
        for pr in range(SSD_HEADS_PER_GROUP // 2):
            h0 = g * SSD_HEADS_PER_GROUP + 2 * pr
            ch = slice(h0 * SSD_HEAD_DIM, h0 * SSD_HEAD_DIM + LANES)
            m_parts = []
            a_cols = []
            for h in (h0, h0 + 1):
                a_col = jnp.broadcast_to(acum[:, h:h + 1], (ln, LANES))
                seg = a_col - acum_t[h:h + 1, :]
                m_parts.append((cb * jnp.where(causal, jnp.exp(seg), 0.0)).astype(BF16))
                a_cols.append(a_col)
            a_pair = jnp.where(lane_lo, a_cols[0], a_cols[1])
            dt_pair = jnp.where(lane_lo, dt[:, h0:h0 + 1], dt[:, h0 + 1:h0 + 2])
            x_pair = xs_ref[:, ch]
            xdt = x_pair * dt_pair
            xdt_b = xdt.astype(BF16)
            zero = jnp.zeros_like(xdt_b)
            rhs = jnp.concatenate([jnp.where(lane_lo, xdt_b, zero), jnp.where(lane_lo, zero, xdt_b)], axis=0)
            y = _dot(jnp.concatenate(m_parts, axis=1), rhs)
            y = y + y_state[:, pr * LANES:(pr + 1) * LANES] * jnp.exp(a_pair)
            y_ref[:, ch] = y + x_pair * dskip_ref[:, ch]
            a_last = a_pair[ln - 1:ln, :]
            xw_parts.append((xdt * jnp.exp(a_last - a_pair)).astype(BF16))
            decay_parts.append(jnp.exp(a_last))
        xw = jnp.concatenate(xw_parts, axis=1)
        decay = jnp.concatenate(decay_parts, axis=1)
        state_ref[g] = state_ref[g] * decay + _dot_tn(b_g, xw)

    gated = jnp.concatenate([y_ref[:, 0:cbw] * _silu(z0_ref[...]), y_ref[:, cbw:2 * cbw] * _silu(z1_ref[...])], axis=1)
    o_ref[...] = _rms(gated, ng_ref[...]).astype(o_ref.dtype)


def _ssd_scan(zx, dt_raw, conv_w, conv_b, dt_bias, a_log, d_skip, norm_g):
    s = zx.shape[0]
    ln = SSD_CHUNK
    cbw = SSD_COL_BLOCK
    conv_dim = conv_w.shape[1]
    assert SSD_INNER == 2 * cbw and conv_dim == 3 * cbw

    def col_block(j):
        return pl.BlockSpec((ln, cbw), lambda c: (c, j))

    def whole(shape):
        return pl.BlockSpec(shape, lambda c: (0,) * len(shape))

    return pl.pallas_call(
        _ssd_kernel,
        grid=(s // ln,),
        in_specs=[
            col_block(0), col_block(1), col_block(2), col_block(3), col_block(4),
            pl.BlockSpec((ln, LANES), lambda c: (c, 0)),
            whole((SSD_CONV, conv_dim)), whole((1, conv_dim)),
            whole((1, SSD_HEADS)), whole((1, SSD_HEADS)),
            whole((1, SSD_INNER)), whole((1, SSD_INNER)),
        ],
        out_specs=pl.BlockSpec((ln, SSD_INNER), lambda c: (c, 0)),
        out_shape=jax.ShapeDtypeStruct((s, SSD_INNER), BF16),
        scratch_shapes=[
            pltpu.VMEM((3, CONV_HALO + ln, cbw), F32),
            pltpu.VMEM((ln, SSD_INNER), F32),
            pltpu.VMEM((ln, 2 * SSD_GROUPS * SSD_STATE), BF16),
            pltpu.VMEM((ln, SSD_INNER), F32),
            pltpu.VMEM((SSD_GROUPS, SSD_STATE, SSD_GROUP_WIDTH), F32),
        ],
        compiler_params=_params("arbitrary"),
        name="ssd_scan",
    )(zx, zx, zx, zx, zx, dt_raw, conv_w, conv_b.reshape(1, conv_dim),
      dt_bias.reshape(1, SSD_HEADS), a_log.reshape(1, SSD_HEADS),
      jnp.repeat(d_skip, SSD_HEAD_DIM).reshape(1, SSD_INNER), norm_g.reshape(1, SSD_INNER))


def _head_rms(y, gain, n_heads):
    return [_rms(y[:, h * XA_HEAD_DIM:(h + 1) * XA_HEAD_DIM], gain) for h in range(n_heads)]


def _mem_kv_kernel(mem_ref, g_ref, w_ref, kg_ref, k_ref, v_ref):
    xw = XA_HEADS * XA_HEAD_DIM
    kv = _dot(_rms(mem_ref[...], g_ref[...]).astype(BF16), w_ref[...])
    k_ref[...] = jnp.concatenate(_head_rms(kv[:, 0:xw], kg_ref[...], XA_HEADS), axis=1).astype(BF16)
    v_ref[...] = kv[:, xw:2 * xw].astype(BF16)


def _mem_kv(mem, g, w_kv, k_gain):
    m, d = mem.shape
    xw = XA_HEADS * XA_HEAD_DIM
    return pl.pallas_call(
        _mem_kv_kernel,
        out_shape=(jax.ShapeDtypeStruct((m, xw), BF16), jax.ShapeDtypeStruct((m, xw), BF16)),
        compiler_params=pltpu.CompilerParams(vmem_limit_bytes=VMEM_LIMIT_BYTES),
        name="mem_kv",
    )(mem, g.reshape(1, d), w_kv, k_gain.reshape(1, XA_HEAD_DIM))


def _xattn_kernel(x_ref, g_ref, wq_ref, qg_ref, k_ref, v_ref, wo_ref, o_ref):
    x = x_ref[...]
    q = _dot(_rms(x, g_ref[...]).astype(BF16), wq_ref[...])
    heads = []
    for h, qh in enumerate(_head_rms(q, qg_ref[...], XA_HEADS)):
        cols = slice(h * XA_HEAD_DIM, (h + 1) * XA_HEAD_DIM)
        s = _dot_nt((qh * XA_HEAD_DIM ** -0.5).astype(BF16), k_ref[:, cols])
        p = jnp.exp(s - jnp.max(s, axis=-1, keepdims=True))
        p = p / jnp.sum(p, axis=-1, keepdims=True)
        heads.append(_dot(p.astype(BF16), v_ref[:, cols]))
    o = jnp.concatenate(heads, axis=1).astype(BF16)
    o_ref[...] = x + _dot(o, wo_ref[...])


def _xattn(x, g, w_q, q_gain, k, v, w_o, *, tm=512):
    s, d = x.shape
    m, xw = k.shape
    tm = min(tm, s)

    def whole(shape):
        return pl.BlockSpec(shape, lambda i: (0,) * len(shape))

    return pl.pallas_call(
        _xattn_kernel,
        grid=(s // tm,),
        in_specs=[
            pl.BlockSpec((tm, d), lambda i: (i, 0)),
            whole((1, d)), whole((d, xw)), whole((1, XA_HEAD_DIM)),
            whole((m, xw)), whole((m, xw)), whole((xw, d)),
        ],
        out_specs=pl.BlockSpec((tm, d), lambda i: (i, 0)),
        out_shape=jax.ShapeDtypeStruct((s, d), F32),
        compiler_params=_params("parallel"),
        name="mem_xattn",
    )(x, g.reshape(1, d), w_q, q_gain.reshape(1, XA_HEAD_DIM), k, v, w_o)


def _attn_layer(x, norm_g, w_qkv, w_o, q_gain, k_gain, lam_vecs, subln_gain, bias, lambda_init, t):
    d = x.shape[1]
    n_maps = 2 * DA_HEADS
    col_gain = jnp.concatenate([jnp.tile(q_gain * DA_HEAD_DIM ** -0.5, n_maps), jnp.tile(k_gain, n_maps),
                                jnp.ones((d,), F32)]).reshape(1, 3 * d)
    qkv = _proj(x, norm_g, w_qkv.astype(BF16), col_gain, n_cols=3 * d, n_norm_cols=2 * d, out_dtype=BF16)
    o = _diff_attention(qkv, bias, lam_vecs, subln_gain, lambda_init, t=t)
    return _out_proj(o, w_o.astype(BF16), x)


def _ssd_layer(x, norm_g, w_in, conv_w, conv_b, dt_bias, a_log, d_skip, ssd_norm, w_out):
    d = x.shape[1]
    n_main = 2 * SSD_INNER + 2 * SSD_GROUPS * SSD_STATE
    w_in_b = w_in.astype(BF16)
    w_dt = jnp.pad(w_in_b[:, n_main:], ((0, 0), (0, LANES - SSD_HEADS)))
    zx = _proj(x, norm_g, w_in_b, jnp.ones((1, n_main), F32), n_cols=n_main, n_norm_cols=0, out_dtype=F32)
    dt_raw = _proj(x, norm_g, w_dt, jnp.ones((1, LANES), F32), n_cols=LANES, n_norm_cols=0, out_dtype=F32)
    y = _ssd_scan(zx, dt_raw, conv_w, conv_b, dt_bias, a_log, d_skip, ssd_norm)
    return _out_proj(y, w_out.astype(BF16), x)


@jax.jit
def kernel(x, mem, rel_bias, ffn1_norm, ffn1_w_gate, ffn1_w_up, ffn1_w_down, mix_norm, attn_w_qkv, attn_w_o,
           attn_q_norm, attn_k_norm, attn_lambda, attn_subln, ssd_w_in, ssd_conv_w, ssd_conv_b, ssd_dt_bias,
           ssd_a_log, ssd_d, ssd_norm, ssd_w_out, xattn_norm, mem_norm, xattn_w_q, xattn_w_kv, xattn_w_o,
           xattn_q_norm, xattn_k_norm, ffn2_norm, ffn2_w_gate, ffn2_w_up, ffn2_w_down):
    b, s, d = x.shape
    assert b == 1
    t = min(ATTN_BLOCK, s)
    xs = x[0]
    mem2 = mem[0]
    bias = _bias_tiles(rel_bias, t)
    for i in range(DEPTH):
        xs = _ffn(xs, ffn1_norm[i], ffn1_w_gate[i].astype(BF16), ffn1_w_up[i].astype(BF16),
                  ffn1_w_down[i].astype(BF16))
        j = i // N_MIXERS
        if i % N_MIXERS == 0:
            lambda_init = 0.8 - 0.6 * math.exp(-0.3 * i)
            xs = _attn_layer(xs, mix_norm[i], attn_w_qkv[j], attn_w_o[j], attn_q_norm[j], attn_k_norm[j],
                             attn_lambda[j], attn_subln[j], bias, lambda_init, t)
        else:
            xs = _ssd_layer(xs, mix_norm[i], ssd_w_in[j], ssd_conv_w[j], ssd_conv_b[j], ssd_dt_bias[j],
                            ssd_a_log[j], ssd_d[j], ssd_norm[j], ssd_w_out[j])
        k, v = _mem_kv(mem2, mem_norm[i], xattn_w_kv[i].astype(BF16), xattn_k_norm[i])
        xs = _xattn(xs, xattn_norm[i], xattn_w_q[i].astype(BF16), xattn_q_norm[i], k, v, xattn_w_o[i].astype(BF16))
        xs = _ffn(xs, ffn2_norm[i], ffn2_w_gate[i].astype(BF16), ffn2_w_up[i].astype(BF16),
                  ffn2_w_down[i].astype(BF16))
    return xs[None]
```

```python
import functools
import math

import jax
import jax.numpy as jnp
from jax import lax
from jax.experimental import pallas as pl
from jax.experimental.pallas import tpu as pltpu

F32 = jnp.float32
BF16 = jnp.bfloat16

EPS = 1e-6
NEG_INF = -1e30
LOG2E = math.log2(math.e)

DEPTH = 4
N_MIXERS = 2

DA_HEADS = 8
DA_HEAD_DIM = 128
DA_V_DIM = 2 * DA_HEAD_DIM
REL_BUCKETS = 32
REL_MAX_DIST = 128
ATTN_BLOCK = 512
MAX_UNSHIFTED_LOG2_LOGIT = 100.0

SSD_HEAD_DIM = 64
SSD_HEADS = 64
SSD_GROUPS = 8
SSD_STATE = 128
SSD_CONV = 4
SSD_CHUNK = 128
SSD_INNER = SSD_HEADS * SSD_HEAD_DIM
SSD_HEADS_PER_GROUP = SSD_HEADS // SSD_GROUPS
SSD_GROUP_WIDTH = SSD_HEADS_PER_GROUP * SSD_HEAD_DIM
SSD_COL_BLOCK = 2048
CONV_HALO = 8

XA_HEADS = 4
XA_HEAD_DIM = 128

LANES = 128
VMEM_LIMIT_BYTES = 56 * 1024 * 1024


def _params(*sem):
    return pltpu.CompilerParams(dimension_semantics=sem, vmem_limit_bytes=VMEM_LIMIT_BYTES)


def _rms(x, g):
    return x * lax.rsqrt(jnp.mean(x * x, axis=-1, keepdims=True) + EPS) * g


def _silu(x):
    return x * jax.nn.sigmoid(x)


def _dot(a, b):
    return jnp.dot(a, b, preferred_element_type=F32)


def _dot_nt(a, b):
    return lax.dot_general(a, b, (((1,), (1,)), ((), ())), preferred_element_type=F32)


def _dot_tn(a, b):
    return lax.dot_general(a, b, (((0,), (0,)), ((), ())), preferred_element_type=F32)


def _ffn_kernel(x_ref, g_ref, wg_ref, wu_ref, wd_ref, o_ref, h_ref):
    j = pl.program_id(1)

    @pl.when(j == 0)
    def _():
        h_ref[...] = _rms(x_ref[...], g_ref[...]).astype(BF16)
        o_ref[...] = jnp.zeros_like(o_ref)

    h = h_ref[...]
    gate = _dot(h, wg_ref[...])
    up = _dot(h, wu_ref[...])
    a = (_silu(gate) * up).astype(BF16)
    o_ref[...] += _dot(a, wd_ref[...])

    @pl.when(j == pl.num_programs(1) - 1)
    def _():
        o_ref[...] = x_ref[...] + 0.5 * o_ref[...]


def _ffn(x, g, wg, wu, wd, *, tm=512, tf=512):
    s, d = x.shape
    f = wg.shape[1]
    tm = min(tm, s)
    return pl.pallas_call(
        _ffn_kernel,
        grid=(s // tm, f // tf),
        in_specs=[
            pl.BlockSpec((tm, d), lambda i, j: (i, 0)),
            pl.BlockSpec((1, d), lambda i, j: (0, 0)),
            pl.BlockSpec((d, tf), lambda i, j: (0, j)),
            pl.BlockSpec((d, tf), lambda i, j: (0, j)),
            pl.BlockSpec((tf, d), lambda i, j: (j, 0)),
        ],
        out_specs=pl.BlockSpec((tm, d), lambda i, j: (i, 0)),
        out_shape=jax.ShapeDtypeStruct((s, d), F32),
        scratch_shapes=[pltpu.VMEM((tm, d), BF16)],
        compiler_params=_params("parallel", "arbitrary"),
        name="ffn",
    )(x, g.reshape(1, d), wg, wu, wd)


def _proj_kernel(x_ref, g_ref, w_ref, cg_ref, o_ref, h_ref, *, n_norm_tiles):
    j = pl.program_id(1)

    @pl.when(j == 0)
    def _():
        h_ref[...] = _rms(x_ref[...], g_ref[...]).astype(BF16)

    y = _dot(h_ref[...], w_ref[...])

    if n_norm_tiles:
        @pl.when(j < n_norm_tiles)
        def _():
            for c in range(y.shape[1] // LANES):
                cols = slice(c * LANES, (c + 1) * LANES)
                o_ref[:, cols] = _rms(y[:, cols], cg_ref[:, cols]).astype(o_ref.dtype)

        @pl.when(j >= n_norm_tiles)
        def _():
            o_ref[...] = y.astype(o_ref.dtype)
    else:
        o_ref[...] = y.astype(o_ref.dtype)


def _proj(x, g, w, col_gain, *, n_cols, n_norm_cols, out_dtype, tm=512, tn=512):
    s, d = x.shape
    tm = min(tm, s)
    tn = min(tn, n_cols)
    assert n_cols % tn == 0 and n_norm_cols % tn == 0
    return pl.pallas_call(
        functools.partial(_proj_kernel, n_norm_tiles=n_norm_cols // tn),
        grid=(s // tm, n_cols // tn),
        in_specs=[
            pl.BlockSpec((tm, d), lambda i, j: (i, 0)),
            pl.BlockSpec((1, d), lambda i, j: (0, 0)),
            pl.BlockSpec((d, tn), lambda i, j: (0, j)),
            pl.BlockSpec((1, tn), lambda i, j: (0, j)),
        ],
        out_specs=pl.BlockSpec((tm, tn), lambda i, j: (i, j)),
        out_shape=jax.ShapeDtypeStruct((s, n_cols), out_dtype),
        scratch_shapes=[pltpu.VMEM((tm, d), BF16)],
        compiler_params=_params("parallel", "arbitrary"),
        name="proj",
    )(x, g.reshape(1, d), w, col_gain)


def _out_proj_kernel(a_ref, w_ref, x_ref, o_ref):
    o_ref[...] = x_ref[...] + _dot(a_ref[...], w_ref[...])


def _out_proj(a, w, x, *, tm=512, tn=512):
    s, k = a.shape
    d = w.shape[1]
    tm = min(tm, s)
    return pl.pallas_call(
        _out_proj_kernel,
        grid=(s // tm, d // tn),
        in_specs=[
            pl.BlockSpec((tm, k), lambda i, j: (i, 0)),
            pl.BlockSpec((k, tn), lambda i, j: (0, j)),
            pl.BlockSpec((tm, tn), lambda i, j: (i, j)),
        ],
        out_specs=pl.BlockSpec((tm, tn), lambda i, j: (i, j)),
        out_shape=jax.ShapeDtypeStruct((s, d), F32),
        compiler_params=_params("parallel", "arbitrary"),
        name="out_proj",
    )(a, w, x)


def _bias_kernel(tab_ref, o_ref, *, t):
    h = pl.program_id(0)
    row = lax.broadcasted_iota(jnp.int32, (t, t), 0)
    col = lax.broadcasted_iota(jnp.int32, (t, t), 1)
    max_exact = REL_BUCKETS // 2
    far = tab_ref[REL_BUCKETS - 1, h]
    for blk in range(2):
        dist = row - col + blk * t
        n = jnp.maximum(dist, 0)
        nf = jnp.maximum(n, 1).astype(F32)
        large = max_exact + (jnp.log(nf / max_exact) / math.log(REL_MAX_DIST / max_exact)
                             * (REL_BUCKETS - max_exact)).astype(jnp.int32)
        large = jnp.minimum(large, REL_BUCKETS - 1)
        bucket = jnp.where(n < max_exact, n, large)
        bias = jnp.zeros((t, t), F32)
        for b in range(REL_BUCKETS - 1):
            bias = jnp.where(bucket == b, (tab_ref[b, h] - far) * LOG2E, bias)
        if blk == 0:
            bias = jnp.where(dist >= 0, bias, NEG_INF)
        o_ref[0, blk] = bias


def _bias_tiles(rel_bias, t):
    assert t >= REL_MAX_DIST
    return pl.pallas_call(
        functools.partial(_bias_kernel, t=t),
        grid=(DA_HEADS,),
        in_specs=[pl.BlockSpec(memory_space=pltpu.SMEM)],
        out_specs=pl.BlockSpec((1, 2, t, t), lambda h: (h, 0, 0, 0)),
        out_shape=jax.ShapeDtypeStruct((DA_HEADS, 2, t, t), F32),
        compiler_params=_params("arbitrary"),
        name="rel_bias_tiles",
    )(rel_bias)


def _attn_kernel(q_ref, k_ref, v_ref, bias_ref, lv_ref, sg_ref, o_ref, m_ref, l_ref, acc_ref, *, t, lambda_init,
                 online):
    qb = pl.program_id(1)
    dh = DA_HEAD_DIM
    if online:
        m_ref[...] = jnp.full(m_ref.shape, NEG_INF, F32)
    l_ref[...] = jnp.zeros(l_ref.shape, F32)
    acc_ref[...] = jnp.zeros(acc_ref.shape, F32)

    def lane_chunk_sum(p):
        out = p[:, 0:LANES]
        for c in range(1, t // LANES):
            out = out + p[:, c * LANES:(c + 1) * LANES]
        return out

    def block(start, bias):
        v = v_ref[pl.ds(start, t), :]
        for mp in range(2):
            cols = slice(mp * dh, (mp + 1) * dh)
            s = _dot_nt(q_ref[:, cols], k_ref[pl.ds(start, t), cols])
            if bias is not None:
                s = s + bias
            if online:
                m_prev = m_ref[mp]
                m_new = jnp.maximum(m_prev, jnp.max(s, axis=-1, keepdims=True))
                alpha = jnp.exp2(m_prev - m_new)
                p = jnp.exp2(s - m_new)
                l_ref[mp] = alpha * l_ref[mp] + lane_chunk_sum(p)
                acc_ref[mp] = alpha * acc_ref[mp] + _dot(p.astype(BF16), v)
                m_ref[mp] = m_new
            else:
                p = jnp.exp2(s)
                l_ref[mp] += lane_chunk_sum(p)
                acc_ref[mp] += _dot(p.astype(BF16), v)

    def far_block(kb, carry):
        block(pl.multiple_of(kb * t, t), None)
        return carry

    lax.fori_loop(0, jnp.maximum(qb - 1, 0), far_block, 0)

    @pl.when(qb > 0)
    def _():
        block(pl.multiple_of((qb - 1) * t, t), bias_ref[0, 1])

    block(pl.multiple_of(qb * t, t), bias_ref[0, 0])

    lv = lv_ref[...]
    lam = (jnp.exp(jnp.sum(lv[0:1] * lv[1:2], axis=-1, keepdims=True))
           - jnp.exp(jnp.sum(lv[2:3] * lv[3:4], axis=-1, keepdims=True)) + lambda_init)
    l1 = jnp.sum(l_ref[0], axis=-1, keepdims=True)
    l2 = jnp.sum(l_ref[1], axis=-1, keepdims=True)
    o = acc_ref[0] / l1 - lam * (acc_ref[1] / l2)
    o_ref[...] = (_rms(o, sg_ref[...]) * (1.0 - lambda_init)).astype(o_ref.dtype)


def _diff_attention(qkv, bias, lam_vecs, subln_gain, lambda_init, logit_bound, *, t):
    s = qkv.shape[0]
    t = min(t, s)
    hw = DA_V_DIM

    def call(online):
        return pl.pallas_call(
            functools.partial(_attn_kernel, t=t, lambda_init=lambda_init, online=online),
            grid=(DA_HEADS, s // t),
            in_specs=[
                pl.BlockSpec((t, hw), lambda h, i: (i, h)),
                pl.BlockSpec((s, hw), lambda h, i: (0, DA_HEADS + h)),
                pl.BlockSpec((s, hw), lambda h, i: (0, 2 * DA_HEADS + h)),
                pl.BlockSpec((1, 2, t, t), lambda h, i: (h, 0, 0, 0)),
                pl.BlockSpec((4, DA_HEAD_DIM), lambda h, i: (0, 0)),
                pl.BlockSpec((1, hw), lambda h, i: (0, 0)),
            ],
            out_specs=pl.BlockSpec((t, hw), lambda h, i: (i, h)),
            out_shape=jax.ShapeDtypeStruct((s, DA_HEADS * hw), BF16),
            scratch_shapes=[
                pltpu.VMEM((2, t, 1), F32),
                pltpu.VMEM((2, t, LANES), F32),
                pltpu.VMEM((2, t, hw), F32),
            ],
            compiler_params=_params("parallel", "arbitrary"),
            name="diff_attn_online" if online else "diff_attn",
        )

    operands = (qkv, qkv, qkv, bias, lam_vecs, subln_gain.reshape(1, hw))
    return lax.cond(logit_bound < MAX_UNSHIFTED_LOG2_LOGIT,
                    lambda *a: call(False)(*a), lambda *a: call(True)(*a), *operands)


def _ssd_kernel(z0_ref, z1_ref, x0_ref, x1_ref, bc_ref, dt_ref, cw_ref, cb_ref, dtb_ref, alog_ref, dskip_ref,
                ng_ref, o_ref, ext_ref, xs_ref, bcs_ref, y_ref, state_ref):
    c = pl.program_id(0)
    ln = SSD_CHUNK
    cbw = SSD_COL_BLOCK
    ns = SSD_STATE
    gw = SSD_GROUP_WIDTH

    @pl.when(c == 0)
    def _():
        ext_ref[:, 0:CONV_HALO, :] = jnp.zeros((3, CONV_HALO, cbw), F32)
        state_ref[...] = jnp.zeros(state_ref.shape, F32)

    for blk, src in enumerate((x0_ref, x1_ref, bc_ref)):
        cols = slice(blk * cbw, (blk + 1) * cbw)
        ext_ref[blk, CONV_HALO:CONV_HALO + ln, :] = src[...]
        acc = cb_ref[:, cols] + cw_ref[0:1, cols] * ext_ref[blk, pl.ds(CONV_HALO - SSD_CONV + 1, ln), :]
        for k in range(1, SSD_CONV):
            acc = acc + cw_ref[k:k + 1, cols] * ext_ref[blk, pl.ds(CONV_HALO - SSD_CONV + 1 + k, ln), :]
        ext_ref[blk, 0:CONV_HALO, :] = ext_ref[blk, ln:ln + CONV_HALO, :]
        act = _silu(acc)
        if blk < 2:
            xs_ref[:, cols] = act
        else:
            bcs_ref[...] = act.astype(BF16)

    dt = jax.nn.softplus(dt_ref[:, 0:SSD_HEADS] + dtb_ref[...])
    a = dt * (-jnp.exp(alog_ref[...]))
    row = lax.broadcasted_iota(jnp.int32, (ln, ln), 0)
    col = lax.broadcasted_iota(jnp.int32, (ln, ln), 1)
    causal = row >= col
    acum = jnp.dot(causal.astype(F32), a, preferred_element_type=F32, precision=lax.Precision.HIGHEST)
    acum_t = acum.T
    lane_lo = lax.broadcasted_iota(jnp.int32, (ln, LANES), 1) < SSD_HEAD_DIM

    for g in range(SSD_GROUPS):
        b_g = bcs_ref[:, g * ns:(g + 1) * ns]
        c_g = bcs_ref[:, SSD_GROUPS * ns + g * ns:SSD_GROUPS * ns + (g + 1) * ns]
        cb = _dot_nt(c_g, b_g)
        y_state = _dot(c_g, state_ref[g].astype(BF16))
        xw_parts = []
        decay_parts = []
        for pr in range(SSD_HEADS_PER_GROUP // 2):
            h0 = g * SSD_HEADS_PER_GROUP + 2 * pr
            ch = slice(h0 * SSD_HEAD_DIM, h0 * SSD_HEAD_DIM + LANES)
            m_parts = []
            a_cols = []
            for h in (h0, h0 + 1):
                a_col = jnp.broadcast_to(acum[:, h:h + 1], (ln, LANES))
                seg = a_col - acum_t[h:h + 1, :]
                m_parts.append((cb * jnp.where(causal, jnp.exp(seg), 0.0)).astype(BF16))
                a_cols.append(a_col)
            a_pair = jnp.where(lane_lo, a_cols[0], a_cols[1])
            dt_pair = jnp.where(lane_lo, dt[:, h0:h0 + 1], dt[:, h0 + 1:h0 + 2])
            x_pair = xs_ref[:, ch]
            xdt = x_pair * dt_pair
            xdt_b = xdt.astype(BF16)
            zero = jnp.zeros_like(xdt_b)
            rhs = jnp.concatenate([jnp.where(lane_lo, xdt_b, zero), jnp.where(lane_lo, zero, xdt_b)], axis=0)
            y = _dot(jnp.concatenate(m_parts, axis=1), rhs)
            y = y + y_state[:, pr * LANES:(pr + 1) * LANES] * jnp.exp(a_pair)
            y_ref[:, ch] = y + x_pair * dskip_ref[:, ch]
            a_last = a_pair[ln - 1:ln, :]
            xw_parts.append((xdt * jnp.exp(a_last - a_pair)).astype(BF16))
            decay_parts.append(jnp.exp(a_last))
        xw = jnp.concatenate(xw_parts, axis=1)
        decay = jnp.concatenate(decay_parts, axis=1)
        state_ref[g] = state_ref[g] * decay + _dot_tn(b_g, xw)

    gated = jnp.concatenate([y_ref[:, 0:cbw] * _silu(z0_ref[...]), y_ref[:, cbw:2 * cbw] * _silu(z1_ref[...])], axis=1)
    o_ref[...] = _rms(gated, ng_ref[...]).astype(o_ref.dtype)


def _ssd_scan(zx, dt_raw, conv_w, conv_b, dt_bias, a_log, d_skip, norm_g):
    s = zx.shape[0]
    ln = SSD_CHUNK
    cbw = SSD_COL_BLOCK
    conv_dim = conv_w.shape[1]
    assert SSD_INNER == 2 * cbw and conv_dim == 3 * cbw

    def col_block(j):
        return pl.BlockSpec((ln, cbw), lambda c: (c, j))

    def whole(shape):
        return pl.BlockSpec(shape, lambda c: (0,) * len(shape))

    return pl.pallas_call(
        _ssd_kernel,
        grid=(s // ln,),
        in_specs=[
            col_block(0), col_block(1), col_block(2), col_block(3), col_block(4),
            pl.BlockSpec((ln, LANES), lambda c: (c, 0)),
            whole((SSD_CONV, conv_dim)), whole((1, conv_dim)),
            whole((1, SSD_HEADS)), whole((1, SSD_HEADS)),
            whole((1, SSD_INNER)), whole((1, SSD_INNER)),
        ],
        out_specs=pl.BlockSpec((ln, SSD_INNER), lambda c: (c, 0)),
        out_shape=jax.ShapeDtypeStruct((s, SSD_INNER), BF16),
        scratch_shapes=[
            pltpu.VMEM((3, CONV_HALO + ln, cbw), F32),
            pltpu.VMEM((ln, SSD_INNER), F32),
            pltpu.VMEM((ln, 2 * SSD_GROUPS * SSD_STATE), BF16),
            pltpu.VMEM((ln, SSD_INNER), F32),
            pltpu.VMEM((SSD_GROUPS, SSD_STATE, SSD_GROUP_WIDTH), F32),
        ],
        compiler_params=_params("arbitrary"),
        name="ssd_scan",
    )(zx, zx, zx, zx, zx, dt_raw, conv_w, conv_b.reshape(1, conv_dim),
      dt_bias.reshape(1, SSD_HEADS), a_log.reshape(1, SSD_HEADS),
      jnp.repeat(d_skip, SSD_HEAD_DIM).reshape(1, SSD_INNER), norm_g.reshape(1, SSD_INNER))


def _head_rms(y, gain, n_heads):
    return [_rms(y[:, h * XA_HEAD_DIM:(h + 1) * XA_HEAD_DIM], gain) for h in range(n_heads)]


def _mem_kv_kernel(mem_ref, g_ref, w_ref, kg_ref, k_ref, v_ref):
    xw = XA_HEADS * XA_HEAD_DIM
    kv = _dot(_rms(mem_ref[...], g_ref[...]).astype(BF16), w_ref[...])
    k_ref[...] = jnp.concatenate(_head_rms(kv[:, 0:xw], kg_ref[...], XA_HEADS), axis=1).astype(BF16)
    v_ref[...] = kv[:, xw:2 * xw].astype(BF16)


def _mem_kv(mem, g, w_kv, k_gain):
    m, d = mem.shape
    xw = XA_HEADS * XA_HEAD_DIM
    return pl.pallas_call(
        _mem_kv_kernel,
        out_shape=(jax.ShapeDtypeStruct((m, xw), BF16), jax.ShapeDtypeStruct((m, xw), BF16)),
        compiler_params=pltpu.CompilerParams(vmem_limit_bytes=VMEM_LIMIT_BYTES),
        name="mem_kv",
    )(mem, g.reshape(1, d), w_kv, k_gain.reshape(1, XA_HEAD_DIM))


def _xattn_kernel(x_ref, g_ref, wq_ref, qg_ref, k_ref, v_ref, wo_ref, o_ref):
    x = x_ref[...]
    q = _dot(_rms(x, g_ref[...]).astype(BF16), wq_ref[...])
    heads = []
    for h, qh in enumerate(_head_rms(q, qg_ref[...], XA_HEADS)):
        cols = slice(h * XA_HEAD_DIM, (h + 1) * XA_HEAD_DIM)
        s = _dot_nt((qh * XA_HEAD_DIM ** -0.5).astype(BF16), k_ref[:, cols])
        p = jnp.exp(s - jnp.max(s, axis=-1, keepdims=True))
        p = p / jnp.sum(p, axis=-1, keepdims=True)
        heads.append(_dot(p.astype(BF16), v_ref[:, cols]))
    o = jnp.concatenate(heads, axis=1).astype(BF16)
    o_ref[...] = x + _dot(o, wo_ref[...])


def _xattn(x, g, w_q, q_gain, k, v, w_o, *, tm=512):
    s, d = x.shape
    m, xw = k.shape
    tm = min(tm, s)

    def whole(shape):
        return pl.BlockSpec(shape, lambda i: (0,) * len(shape))

    return pl.pallas_call(
        _xattn_kernel,
        grid=(s // tm,),
        in_specs=[
            pl.BlockSpec((tm, d), lambda i: (i, 0)),
            whole((1, d)), whole((d, xw)), whole((1, XA_HEAD_DIM)),
            whole((m, xw)), whole((m, xw)), whole((xw, d)),
        ],
        out_specs=pl.BlockSpec((tm, d), lambda i: (i, 0)),
        out_shape=jax.ShapeDtypeStruct((s, d), F32),
        compiler_params=_params("parallel"),
        name="mem_xattn",
    )(x, g.reshape(1, d), w_q, q_gain.reshape(1, XA_HEAD_DIM), k, v, w_o)


def _attn_layer(x, norm_g, w_qkv, w_o, q_gain, k_gain, lam_vecs, subln_gain, rel_bias, bias, lambda_init, t):
    d = x.shape[1]
    n_maps = 2 * DA_HEADS
    q_col_gain = q_gain * (DA_HEAD_DIM ** -0.5 * LOG2E)
    col_gain = jnp.concatenate([jnp.tile(q_col_gain, n_maps), jnp.tile(k_gain, n_maps),
                                jnp.ones((d,), F32)]).reshape(1, 3 * d)
    qkv = _proj(x, norm_g, w_qkv.astype(BF16), col_gain, n_cols=3 * d, n_norm_cols=2 * d, out_dtype=BF16)
    logit_bound = (DA_HEAD_DIM * jnp.max(jnp.abs(q_col_gain)) * jnp.max(jnp.abs(k_gain))
                   + LOG2E * jnp.max(jnp.abs(rel_bias - rel_bias[REL_BUCKETS - 1])))
    o = _diff_attention(qkv, bias, lam_vecs, subln_gain, lambda_init, logit_bound, t=t)
    return _out_proj(o, w_o.astype(BF16), x)


def _ssd_layer(x, norm_g, w_in, conv_w, conv_b, dt_bias, a_log, d_skip, ssd_norm, w_out):
    d = x.shape[1]
    n_main = 2 * SSD_INNER + 2 * SSD_GROUPS * SSD_STATE
    w_in_b = w_in.astype(BF16)
    w_dt = jnp.pad(w_in_b[:, n_main:], ((0, 0), (0, LANES - SSD_HEADS)))
    zx = _proj(x, norm_g, w_in_b, jnp.ones((1, n_main), F32), n_cols=n_main, n_norm_cols=0, out_dtype=F32)
    dt_raw = _proj(x, norm_g, w_dt, jnp.ones((1, LANES), F32), n_cols=LANES, n_norm_cols=0, out_dtype=F32)
    y = _ssd_scan(zx, dt_raw, conv_w, conv_b, dt_bias, a_log, d_skip, ssd_norm)
    return _out_proj(y, w_out.astype(BF16), x)


@jax.jit
def kernel(x, mem, rel_bias, ffn1_norm, ffn1_w_gate, ffn1_w_up, ffn1_w_down, mix_norm, attn_w_qkv, attn_w_o,
           attn_q_norm, attn_k_norm, attn_lambda, attn_subln, ssd_w_in, ssd_conv_w, ssd_conv_b, ssd_dt_bias,
           ssd_a_log, ssd_d, ssd_norm, ssd_w_out, xattn_norm, mem_norm, xattn_w_q, xattn_w_kv, xattn_w_o,
           xattn_q_norm, xattn_k_norm, ffn2_norm, ffn2_w_gate, ffn2_w_up, ffn2_w_down):
    b, s, d = x.shape
    assert b == 1
    t = min(ATTN_BLOCK, s)
    xs = x[0]
    mem2 = mem[0]
    bias = _bias_tiles(rel_bias, t)
    for i in range(DEPTH):
        xs = _ffn(xs, ffn1_norm[i], ffn1_w_gate[i].astype(BF16), ffn1_w_up[i].astype(BF16),
                  ffn1_w_down[i].astype(BF16))
        j = i // N_MIXERS
        if i % N_MIXERS == 0:
            lambda_init = 0.8 - 0.6 * math.exp(-0.3 * i)
            xs = _attn_layer(xs, mix_norm[i], attn_w_qkv[j], attn_w_o[j], attn_q_norm[j], attn_k_norm[j],
                             attn_lambda[j], attn_subln[j], rel_bias, bias, lambda_init, t)
        else:
            xs = _ssd_layer(xs, mix_norm[i], ssd_w_in[j], ssd_conv_w[j], ssd_conv_b[j], ssd_dt_bias[j],
                            ssd_a_log[j], ssd_d[j], ssd_norm[j], ssd_w_out[j])
        k, v = _mem_kv(mem2, mem_norm[i], xattn_w_kv[i].astype(BF16), xattn_k_norm[i])
        xs = _xattn(xs, xattn_norm[i], xattn_w_q[i].astype(BF16), xattn_q_norm[i], k, v, xattn_w_o[i].astype(BF16))
        xs = _ffn(xs, ffn2_norm[i], ffn2_w_gate[i].astype(BF16), ffn2_w_up[i].astype(BF16),
                  ffn2_w_down[i].astype(BF16))
    return xs[None]
```

```python
import functools
import math

import jax
import jax.numpy as jnp
from jax import lax
from jax.experimental import pallas as pl
from jax.experimental.pallas import tpu as pltpu

F32 = jnp.float32
BF16 = jnp.bfloat16

EPS = 1e-6
NEG_INF = -1e30
LOG2E = math.log2(math.e)

DEPTH = 4
N_MIXERS = 2

DA_HEADS = 8
DA_HEAD_DIM = 128
DA_V_DIM = 2 * DA_HEAD_DIM
REL_BUCKETS = 32
REL_MAX_DIST = 128
ATTN_BLOCK = 512
MAX_UNSHIFTED_LOG2_LOGIT = 100.0

SSD_HEAD_DIM = 64
SSD_HEADS = 64
SSD_GROUPS = 8
SSD_STATE = 128
SSD_CONV = 4
SSD_CHUNK = 128
SSD_INNER = SSD_HEADS * SSD_HEAD_DIM
SSD_HEADS_PER_GROUP = SSD_HEADS // SSD_GROUPS
SSD_GROUP_WIDTH = SSD_HEADS_PER_GROUP * SSD_HEAD_DIM
SSD_COL_BLOCK = 2048
CONV_HALO = 8

XA_HEADS = 4
XA_HEAD_DIM = 128

LANES = 128
VMEM_LIMIT_BYTES = 56 * 1024 * 1024


def _params(*sem):
    return pltpu.CompilerParams(dimension_semantics=sem, vmem_limit_bytes=VMEM_LIMIT_BYTES)


def _rms(x, g):
    return x * lax.rsqrt(jnp.mean(x * x, axis=-1, keepdims=True) + EPS) * g


def _silu(x):
    return x * jax.nn.sigmoid(x)


def _dot(a, b):
    return jnp.dot(a, b, preferred_element_type=F32)


def _dot_nt(a, b):
    return lax.dot_general(a, b, (((1,), (1,)), ((), ())), preferred_element_type=F32)


def _dot_tn(a, b):
    return lax.dot_general(a, b, (((0,), (0,)), ((), ())), preferred_element_type=F32)


def _ffn_kernel(x_ref, g_ref, wg_ref, wu_ref, wd_ref, o_ref, h_ref):
    j = pl.program_id(1)

    @pl.when(j == 0)
    def _():
        h_ref[...] = _rms(x_ref[...], g_ref[...]).astype(BF16)
        o_ref[...] = jnp.zeros_like(o_ref)

    h = h_ref[...]
    gate = _dot(h, wg_ref[...].astype(BF16))
    up = _dot(h, wu_ref[...].astype(BF16))
    a = (_silu(gate) * up).astype(BF16)
    o_ref[...] += _dot(a, wd_ref[...].astype(BF16))

    @pl.when(j == pl.num_programs(1) - 1)
    def _():
        o_ref[...] = x_ref[...] + 0.5 * o_ref[...]


def _ffn(x, g, wg, wu, wd, layer, *, tm=1024, tf=256):
    s, d = x.shape
    f = wg.shape[2]
    tm = min(tm, s)
    return pl.pallas_call(
        _ffn_kernel,
        grid=(s // tm, f // tf),
        in_specs=[
            pl.BlockSpec((tm, d), lambda i, j: (i, 0)),
            pl.BlockSpec((1, d), lambda i, j: (0, 0)),
            pl.BlockSpec((None, d, tf), lambda i, j: (layer, 0, j)),
            pl.BlockSpec((None, d, tf), lambda i, j: (layer, 0, j)),
            pl.BlockSpec((None, tf, d), lambda i, j: (layer, j, 0)),
        ],
        out_specs=pl.BlockSpec((tm, d), lambda i, j: (i, 0)),
        out_shape=jax.ShapeDtypeStruct((s, d), F32),
        scratch_shapes=[pltpu.VMEM((tm, d), BF16)],
        compiler_params=_params("parallel", "arbitrary"),
        name="ffn",
    )(x, g.reshape(1, d), wg, wu, wd)


def _proj_kernel(x_ref, g_ref, w_ref, cg_ref, o_ref, h_ref, *, n_norm_tiles):
    j = pl.program_id(1)

    @pl.when(j == 0)
    def _():
        h_ref[...] = _rms(x_ref[...], g_ref[...]).astype(BF16)

    y = _dot(h_ref[...], w_ref[...].astype(BF16))

    if n_norm_tiles:
        @pl.when(j < n_norm_tiles)
        def _():
            for c in range(y.shape[1] // LANES):
                cols = slice(c * LANES, (c + 1) * LANES)
                o_ref[:, cols] = _rms(y[:, cols], cg_ref[:, cols]).astype(o_ref.dtype)

        @pl.when(j >= n_norm_tiles)
        def _():
            o_ref[...] = y.astype(o_ref.dtype)
    else:
        o_ref[...] = y.astype(o_ref.dtype)


def _proj(x, g, w, layer, col_gain, *, n_cols, n_norm_cols, out_dtype, tm=1024, tn=512):
    s, d = x.shape
    tm = min(tm, s)
    tn = min(tn, n_cols)
    assert n_cols % tn == 0 and n_norm_cols % tn == 0
    return pl.pallas_call(
        functools.partial(_proj_kernel, n_norm_tiles=n_norm_cols // tn),
        grid=(s // tm, n_cols // tn),
        in_specs=[
            pl.BlockSpec((tm, d), lambda i, j: (i, 0)),
            pl.BlockSpec((1, d), lambda i, j: (0, 0)),
            pl.BlockSpec((None, d, tn), lambda i, j: (layer, 0, j)),
            pl.BlockSpec((1, tn), lambda i, j: (0, j)),
        ],
        out_specs=pl.BlockSpec((tm, tn), lambda i, j: (i, j)),
        out_shape=jax.ShapeDtypeStruct((s, n_cols), out_dtype),
        scratch_shapes=[pltpu.VMEM((tm, d), BF16)],
        compiler_params=_params("parallel", "arbitrary"),
        name="proj",
    )(x, g.reshape(1, d), w, col_gain)


def _out_proj_kernel(a_ref, w_ref, x_ref, o_ref):
    o_ref[...] = x_ref[...] + _dot(a_ref[...], w_ref[...].astype(BF16))


def _out_proj(a, w, layer, x, *, tm=1024, tn=512):
    s, k = a.shape
    d = w.shape[2]
    tm = min(tm, s)
    return pl.pallas_call(
        _out_proj_kernel,
        grid=(s // tm, d // tn),
        in_specs=[
            pl.BlockSpec((tm, k), lambda i, j: (i, 0)),
            pl.BlockSpec((None, k, tn), lambda i, j: (layer, 0, j)),
            pl.BlockSpec((tm, tn), lambda i, j: (i, j)),
        ],
        out_specs=pl.BlockSpec((tm, tn), lambda i, j: (i, j)),
        out_shape=jax.ShapeDtypeStruct((s, d), F32),
        compiler_params=_params("parallel", "arbitrary"),
        name="out_proj",
    )(a, w, x)


def _bias_kernel(tab_ref, o_ref, *, t):
    h = pl.program_id(0)
    row = lax.broadcasted_iota(jnp.int32, (t, t), 0)
    col = lax.broadcasted_iota(jnp.int32, (t, t), 1)
    max_exact = REL_BUCKETS // 2
    far = tab_ref[REL_BUCKETS - 1, h]
    for blk in range(2):
        dist = row - col + blk * t
        n = jnp.maximum(dist, 0)
        nf = jnp.maximum(n, 1).astype(F32)
        large = max_exact + (jnp.log(nf / max_exact) / math.log(REL_MAX_DIST / max_exact)
                             * (REL_BUCKETS - max_exact)).astype(jnp.int32)
        large = jnp.minimum(large, REL_BUCKETS - 1)
        bucket = jnp.where(n < max_exact, n, large)
        bias = jnp.zeros((t, t), F32)
        for b in range(REL_BUCKETS - 1):
            bias = jnp.where(bucket == b, (tab_ref[b, h] - far) * LOG2E, bias)
        if blk == 0:
            bias = jnp.where(dist >= 0, bias, NEG_INF)
        o_ref[0, :, (1 - blk) * t:(2 - blk) * t] = bias


def _bias_tiles(rel_bias, t):
    assert t >= REL_MAX_DIST
    return pl.pallas_call(
        functools.partial(_bias_kernel, t=t),
        grid=(DA_HEADS,),
        in_specs=[pl.BlockSpec(memory_space=pltpu.SMEM)],
        out_specs=pl.BlockSpec((1, t, 2 * t), lambda h: (h, 0, 0)),
        out_shape=jax.ShapeDtypeStruct((DA_HEADS, t, 2 * t), F32),
        compiler_params=_params("arbitrary"),
        name="rel_bias_tiles",
    )(rel_bias)


def _attn_kernel(q_ref, k_ref, v_ref, bias_ref, lv_ref, sg_ref, o_ref, m_ref, l_ref, acc_ref, *, t, lambda_init,
                 online):
    qb = pl.program_id(1)
    dh = DA_HEAD_DIM
    if online:
        m_ref[...] = jnp.full(m_ref.shape, NEG_INF, F32)
    l_ref[...] = jnp.zeros(l_ref.shape, F32)
    acc_ref[...] = jnp.zeros(acc_ref.shape, F32)

    def lane_chunk_sum(p):
        out = p[:, 0:LANES]
        for c in range(1, p.shape[1] // LANES):
            out = out + p[:, c * LANES:(c + 1) * LANES]
        return out

    def block(start, width, bias):
        v = v_ref[pl.ds(start, width), :]
        for mp in range(2):
            cols = slice(mp * dh, (mp + 1) * dh)
            s = _dot_nt(q_ref[:, cols], k_ref[pl.ds(start, width), cols])
            if bias is not None:
                s = s + bias
            if online:
                m_prev = m_ref[mp]
                m_new = jnp.maximum(m_prev, jnp.max(s, axis=-1, keepdims=True))
                alpha = jnp.exp2(m_prev - m_new)
                p = jnp.exp2(s - m_new)
                l_ref[mp] = alpha * l_ref[mp] + lane_chunk_sum(p)
                acc_ref[mp] = alpha * acc_ref[mp] + _dot(p.astype(BF16), v)
                m_ref[mp] = m_new
            else:
                p = jnp.exp2(s)
                l_ref[mp] += lane_chunk_sum(p)
                acc_ref[mp] += _dot(p.astype(BF16), v)

    n_far = jnp.maximum(qb - 1, 0)

    def far_pair(i, carry):
        block(pl.multiple_of(i * (2 * t), 2 * t), 2 * t, None)
        return carry

    lax.fori_loop(0, n_far // 2, far_pair, 0)

    @pl.when(n_far % 2 == 1)
    def _():
        block(pl.multiple_of((n_far - 1) * t, t), t, None)

    @pl.when(qb > 0)
    def _():
        block(pl.multiple_of((qb - 1) * t, t), 2 * t, bias_ref[0])

    @pl.when(qb == 0)
    def _():
        block(0, t, bias_ref[0, :, t:2 * t])

    lv = lv_ref[...]
    lam = (jnp.exp(jnp.sum(lv[0:1] * lv[1:2], axis=-1, keepdims=True))
           - jnp.exp(jnp.sum(lv[2:3] * lv[3:4], axis=-1, keepdims=True)) + lambda_init)
    l1 = jnp.sum(l_ref[0], axis=-1, keepdims=True)
    l2 = jnp.sum(l_ref[1], axis=-1, keepdims=True)
    o = acc_ref[0] / l1 - lam * (acc_ref[1] / l2)
    o_ref[...] = (_rms(o, sg_ref[...]) * (1.0 - lambda_init)).astype(o_ref.dtype)


def _diff_attention(qkv, bias, lam_vecs, subln_gain, lambda_init, logit_bound, *, t):
    s = qkv.shape[0]
    t = min(t, s)
    hw = DA_V_DIM

    def call(online):
        return pl.pallas_call(
            functools.partial(_attn_kernel, t=t, lambda_init=lambda_init, online=online),
            grid=(DA_HEADS, s // t),
            in_specs=[
                pl.BlockSpec((t, hw), lambda h, i: (i, h)),
                pl.BlockSpec((s, hw), lambda h, i: (0, DA_HEADS + h)),
                pl.BlockSpec((s, hw), lambda h, i: (0, 2 * DA_HEADS + h)),
                pl.BlockSpec((1, t, 2 * t), lambda h, i: (h, 0, 0)),
                pl.BlockSpec((4, DA_HEAD_DIM), lambda h, i: (0, 0)),
                pl.BlockSpec((1, hw), lambda h, i: (0, 0)),
            ],
            out_specs=pl.BlockSpec((t, hw), lambda h, i: (i, h)),
            out_shape=jax.ShapeDtypeStruct((s, DA_HEADS * hw), BF16),
            scratch_shapes=[
                pltpu.VMEM((2, t, 1), F32),
                pltpu.VMEM((2, t, LANES), F32),
                pltpu.VMEM((2, t, hw), F32),
            ],
            compiler_params=_params("parallel", "arbitrary"),
            name="diff_attn_online" if online else "diff_attn",
        )

    operands = (qkv, qkv, qkv, bias, lam_vecs, subln_gain.reshape(1, hw))
    return lax.cond(logit_bound < MAX_UNSHIFTED_LOG2_LOGIT,
                    lambda *a: call(False)(*a), lambda *a: call(True)(*a), *operands)


def _ssd_kernel(z0_ref, z1_ref, x0_ref, x1_ref, bc_ref, dt_ref, cw_ref, cb_ref, dtb_ref, alog_ref, dskip_ref,
                ng_ref, o_ref, ext_ref, xs_ref, bcs_ref, y_ref, state_ref):
    c = pl.program_id(0)
    ln = SSD_CHUNK
    cbw = SSD_COL_BLOCK
    ns = SSD_STATE
    gw = SSD_GROUP_WIDTH

    @pl.when(c == 0)
    def _():
        ext_ref[:, 0:CONV_HALO, :] = jnp.zeros((3, CONV_HALO, cbw), F32)
        state_ref[...] = jnp.zeros(state_ref.shape, F32)

    for blk, src in enumerate((x0_ref, x1_ref, bc_ref)):
        cols = slice(blk * cbw, (blk + 1) * cbw)
        ext_ref[blk, CONV_HALO:CONV_HALO + ln, :] = src[...]
        acc = cb_ref[:, cols] + cw_ref[0:1, cols] * ext_ref[blk, pl.ds(CONV_HALO - SSD_CONV + 1, ln), :]
        for k in range(1, SSD_CONV):
            acc = acc + cw_ref[k:k + 1, cols] * ext_ref[blk, pl.ds(CONV_HALO - SSD_CONV + 1 + k, ln), :]
        ext_ref[blk, 0:CONV_HALO, :] = ext_ref[blk, ln:ln + CONV_HALO, :]
        act = _silu(acc)
        if blk < 2:
            xs_ref[:, cols] = act
        else:
            bcs_ref[...] = act.astype(BF16)

    dt = jax.nn.softplus(dt_ref[:, 0:SSD_HEADS] + dtb_ref[...])
    a = dt * (-jnp.exp(alog_ref[...]))
    row = lax.broadcasted_iota(jnp.int32, (ln, ln), 0)
    col = lax.broadcasted_iota(jnp.int32, (ln, ln), 1)
    causal = row >= col
    acum = jnp.dot(causal.astype(F32), a, preferred_element_type=F32, precision=lax.Precision.HIGHEST)
    acum_t = acum.T
    lane_lo = lax.broadcasted_iota(jnp.int32, (ln, LANES), 1) < SSD_HEAD_DIM

    for g in range(SSD_GROUPS):
        b_g = bcs_ref[:, g * ns:(g + 1) * ns]
        c_g = bcs_ref[:, SSD_GROUPS * ns + g * ns:SSD_GROUPS * ns + (g + 1) * ns]
        cb = _dot_nt(c_g, b_g)
        y_state = _dot(c_g, state_ref[g].astype(BF16))
        xw_parts = []
        decay_parts = []
        for pr in range(SSD_HEADS_PER_GROUP // 2):
            h0 = g * SSD_HEADS_PER_GROUP + 2 * pr
            ch = slice(h0 * SSD_HEAD_DIM, h0 * SSD_HEAD_DIM + LANES)
            m_parts = []
            a_cols = []
            for h in (h0, h0 + 1):
                a_col = jnp.broadcast_to(acum[:, h:h + 1], (ln, LANES))
                seg = a_col - acum_t[h:h + 1, :]
                m_parts.append((cb * jnp.where(causal, jnp.exp(seg), 0.0)).astype(BF16))
                a_cols.append(a_col)
            a_pair = jnp.where(lane_lo, a_cols[0], a_cols[1])
            dt_pair = jnp.where(lane_lo, dt[:, h0:h0 + 1], dt[:, h0 + 1:h0 + 2])
            x_pair = xs_ref[:, ch]
            xdt = x_pair * dt_pair
            xdt_b = xdt.astype(BF16)
            zero = jnp.zeros_like(xdt_b)
            rhs = jnp.concatenate([jnp.where(lane_lo, xdt_b, zero), jnp.where(lane_lo, zero, xdt_b)], axis=0)
            y = _dot(jnp.concatenate(m_parts, axis=1), rhs)
            y = y + y_state[:, pr * LANES:(pr + 1) * LANES] * jnp.exp(a_pair)
            y_ref[:, ch] = y + x_pair * dskip_ref[:, ch]
            a_last = a_pair[ln - 1:ln, :]
            xw_parts.append((xdt * jnp.exp(a_last - a_pair)).astype(BF16))
            decay_parts.append(jnp.exp(a_last))
        xw = jnp.concatenate(xw_parts, axis=1)
        decay = jnp.concatenate(decay_parts, axis=1)
        state_ref[g] = state_ref[g] * decay + _dot_tn(b_g, xw)

    gated = jnp.concatenate([y_ref[:, 0:cbw] * _silu(z0_ref[...]), y_ref[:, cbw:2 * cbw] * _silu(z1_ref[...])], axis=1)
    o_ref[...] = _rms(gated, ng_ref[...]).astype(o_ref.dtype)


def _ssd_scan(zx, dt_raw, conv_w, conv_b, dt_bias, a_log, d_skip, norm_g):
    s = zx.shape[0]
    ln = SSD_CHUNK
    cbw = SSD_COL_BLOCK
    conv_dim = conv_w.shape[1]
    assert SSD_INNER == 2 * cbw and conv_dim == 3 * cbw

    def col_block(j):
        return pl.BlockSpec((ln, cbw), lambda c: (c, j))

    def whole(shape):
        return pl.BlockSpec(shape, lambda c: (0,) * len(shape))

    return pl.pallas_call(
        _ssd_kernel,
        grid=(s // ln,),
        in_specs=[
            col_block(0), col_block(1), col_block(2), col_block(3), col_block(4),
            pl.BlockSpec((ln, LANES), lambda c: (c, 0)),
            whole((SSD_CONV, conv_dim)), whole((1, conv_dim)),
            whole((1, SSD_HEADS)), whole((1, SSD_HEADS)),
            whole((1, SSD_INNER)), whole((1, SSD_INNER)),
        ],
        out_specs=pl.BlockSpec((ln, SSD_INNER), lambda c: (c, 0)),
        out_shape=jax.ShapeDtypeStruct((s, SSD_INNER), BF16),
        scratch_shapes=[
            pltpu.VMEM((3, CONV_HALO + ln, cbw), F32),
            pltpu.VMEM((ln, SSD_INNER), F32),
            pltpu.VMEM((ln, 2 * SSD_GROUPS * SSD_STATE), BF16),
            pltpu.VMEM((ln, SSD_INNER), F32),
            pltpu.VMEM((SSD_GROUPS, SSD_STATE, SSD_GROUP_WIDTH), F32),
        ],
        compiler_params=_params("arbitrary"),
        name="ssd_scan",
    )(zx, zx, zx, zx, zx, dt_raw, conv_w, conv_b.reshape(1, conv_dim),
      dt_bias.reshape(1, SSD_HEADS), a_log.reshape(1, SSD_HEADS),
      jnp.repeat(d_skip, SSD_HEAD_DIM).reshape(1, SSD_INNER), norm_g.reshape(1, SSD_INNER))


def _head_rms(y, gain, n_heads):
    return [_rms(y[:, h * XA_HEAD_DIM:(h + 1) * XA_HEAD_DIM], gain) for h in range(n_heads)]


def _mem_kv_kernel(mem_ref, g_ref, w_ref, kg_ref, k_ref, v_ref):
    xw = XA_HEADS * XA_HEAD_DIM
    kv = _dot(_rms(mem_ref[...], g_ref[...]).astype(BF16), w_ref[...].astype(BF16))
    k_ref[...] = jnp.concatenate(_head_rms(kv[:, 0:xw], kg_ref[...], XA_HEADS), axis=1).astype(BF16)
    v_ref[...] = kv[:, xw:2 * xw].astype(BF16)


def _mem_kv(mem, g, w_kv, layer, k_gain):
    m, d = mem.shape
    xw = XA_HEADS * XA_HEAD_DIM

    def whole(shape):
        return pl.BlockSpec(shape, lambda i: (0,) * len(shape))

    return pl.pallas_call(
        _mem_kv_kernel,
        grid=(1,),
        in_specs=[whole((m, d)), whole((1, d)), pl.BlockSpec((None, d, 2 * xw), lambda i: (layer, 0, 0)),
                  whole((1, XA_HEAD_DIM))],
        out_specs=(whole((m, xw)), whole((m, xw))),
        out_shape=(jax.ShapeDtypeStruct((m, xw), BF16), jax.ShapeDtypeStruct((m, xw), BF16)),
        compiler_params=_params("arbitrary"),
        name="mem_kv",
    )(mem, g.reshape(1, d), w_kv, k_gain.reshape(1, XA_HEAD_DIM))


def _xattn_kernel(x_ref, g_ref, wq_ref, qg_ref, k_ref, v_ref, wo_ref, o_ref, wq_b_ref, wo_b_ref):
    @pl.when(pl.program_id(0) == 0)
    def _():
        wq_b_ref[...] = wq_ref[...].astype(BF16)
        wo_b_ref[...] = wo_ref[...].astype(BF16)

    x = x_ref[...]
    q = _dot(_rms(x, g_ref[...]).astype(BF16), wq_b_ref[...])
    heads = []
    for h, qh in enumerate(_head_rms(q, qg_ref[...], XA_HEADS)):
        cols = slice(h * XA_HEAD_DIM, (h + 1) * XA_HEAD_DIM)
        s = _dot_nt((qh * XA_HEAD_DIM ** -0.5).astype(BF16), k_ref[:, cols])
        p = jnp.exp(s - jnp.max(s, axis=-1, keepdims=True))
        p = p / jnp.sum(p, axis=-1, keepdims=True)
        heads.append(_dot(p.astype(BF16), v_ref[:, cols]))
    o = jnp.concatenate(heads, axis=1).astype(BF16)
    o_ref[...] = x + _dot(o, wo_b_ref[...])


def _xattn(x, g, w_q, q_gain, k, v, w_o, layer, *, tm=512):
    s, d = x.shape
    m, xw = k.shape
    tm = min(tm, s)

    def whole(shape):
        return pl.BlockSpec(shape, lambda i: (0,) * len(shape))

    return pl.pallas_call(
        _xattn_kernel,
        grid=(s // tm,),
        in_specs=[
            pl.BlockSpec((tm, d), lambda i: (i, 0)),
            whole((1, d)), pl.BlockSpec((None, d, xw), lambda i: (layer, 0, 0)), whole((1, XA_HEAD_DIM)),
            whole((m, xw)), whole((m, xw)), pl.BlockSpec((None, xw, d), lambda i: (layer, 0, 0)),
        ],
        out_specs=pl.BlockSpec((tm, d), lambda i: (i, 0)),
        out_shape=jax.ShapeDtypeStruct((s, d), F32),
        scratch_shapes=[pltpu.VMEM((d, xw), BF16), pltpu.VMEM((xw, d), BF16)],
        compiler_params=_params("arbitrary"),
        name="mem_xattn",
    )(x, g.reshape(1, d), w_q, q_gain.reshape(1, XA_HEAD_DIM), k, v, w_o)


def _attn_layer(x, norm_g, w_qkv, w_o, layer, q_gain, k_gain, lam_vecs, subln_gain, rel_bias, bias, lambda_init, t):
    d = x.shape[1]
    n_maps = 2 * DA_HEADS
    q_col_gain = q_gain * (DA_HEAD_DIM ** -0.5 * LOG2E)
    col_gain = jnp.concatenate([jnp.tile(q_col_gain, n_maps), jnp.tile(k_gain, n_maps),
                                jnp.ones((d,), F32)]).reshape(1, 3 * d)
    qkv = _proj(x, norm_g, w_qkv, layer, col_gain, n_cols=3 * d, n_norm_cols=2 * d, out_dtype=BF16)
    logit_bound = (DA_HEAD_DIM * jnp.max(jnp.abs(q_col_gain)) * jnp.max(jnp.abs(k_gain))
                   + LOG2E * jnp.max(jnp.abs(rel_bias - rel_bias[REL_BUCKETS - 1])))
    o = _diff_attention(qkv, bias, lam_vecs, subln_gain, lambda_init, logit_bound, t=t)
    return _out_proj(o, w_o, layer, x)


def _ssd_layer(x, norm_g, w_in, w_out, layer, conv_w, conv_b, dt_bias, a_log, d_skip, ssd_norm):
    n_main = 2 * SSD_INNER + 2 * SSD_GROUPS * SSD_STATE
    w_dt = jnp.pad(w_in[layer, :, n_main:], ((0, 0), (0, LANES - SSD_HEADS)))[None]
    zx = _proj(x, norm_g, w_in, layer, jnp.ones((1, n_main), F32), n_cols=n_main, n_norm_cols=0, out_dtype=F32)
    dt_raw = _proj(x, norm_g, w_dt, 0, jnp.ones((1, LANES), F32), n_cols=LANES, n_norm_cols=0, out_dtype=F32)
    y = _ssd_scan(zx, dt_raw, conv_w, conv_b, dt_bias, a_log, d_skip, ssd_norm)
    return _out_proj(y, w_out, layer, x)


@jax.jit
def kernel(x, mem, rel_bias, ffn1_norm, ffn1_w_gate, ffn1_w_up, ffn1_w_down, mix_norm, attn_w_qkv, attn_w_o,
           attn_q_norm, attn_k_norm, attn_lambda, attn_subln, ssd_w_in, ssd_conv_w, ssd_conv_b, ssd_dt_bias,
           ssd_a_log, ssd_d, ssd_norm, ssd_w_out, xattn_norm, mem_norm, xattn_w_q, xattn_w_kv, xattn_w_o,
           xattn_q_norm, xattn_k_norm, ffn2_norm, ffn2_w_gate, ffn2_w_up, ffn2_w_down):
    b, s, d = x.shape
    assert b == 1
    t = min(ATTN_BLOCK, s)
    xs = x[0]
    mem2 = mem[0]
    bias = _bias_tiles(rel_bias, t)
    for i in range(DEPTH):
        xs = _ffn(xs, ffn1_norm[i], ffn1_w_gate, ffn1_w_up, ffn1_w_down, i)
        j = i // N_MIXERS
        if i % N_MIXERS == 0:
            lambda_init = 0.8 - 0.6 * math.exp(-0.3 * i)
            xs = _attn_layer(xs, mix_norm[i], attn_w_qkv, attn_w_o, j, attn_q_norm[j], attn_k_norm[j],
                             attn_lambda[j], attn_subln[j], rel_bias, bias, lambda_init, t)
        else:
            xs = _ssd_layer(xs, mix_norm[i], ssd_w_in, ssd_w_out, j, ssd_conv_w[j], ssd_conv_b[j], ssd_dt_bias[j],
                            ssd_a_log[j], ssd_d[j], ssd_norm[j])
        k, v = _mem_kv(mem2, mem_norm[i], xattn_w_kv, i, xattn_k_norm[i])
        xs = _xattn(xs, xattn_norm[i], xattn_w_q, xattn_q_norm[i], k, v, xattn_w_o, i)
        xs = _ffn(xs, ffn2_norm[i], ffn2_w_gate, ffn2_w_up, ffn2_w_down, i)
    return xs[None]
```

```python
import functools
import math

import jax
import jax.numpy as jnp
from jax import lax
from jax.experimental import pallas as pl
from jax.experimental.pallas import tpu as pltpu

F32 = jnp.float32
BF16 = jnp.bfloat16

EPS = 1e-6
NEG_INF = -1e30
LOG2E = math.log2(math.e)

DEPTH = 4
N_MIXERS = 2

DA_HEADS = 8
DA_HEAD_DIM = 128
DA_V_DIM = 2 * DA_HEAD_DIM
REL_BUCKETS = 32
REL_MAX_DIST = 128
ATTN_BLOCK = 512
FAR_GROUP = 4
MAX_UNSHIFTED_LOG2_LOGIT = 100.0

SSD_HEAD_DIM = 64
SSD_HEADS = 64
SSD_GROUPS = 8
SSD_STATE = 128
SSD_CONV = 4
SSD_CHUNK = 128
SSD_INNER = SSD_HEADS * SSD_HEAD_DIM
SSD_HEADS_PER_GROUP = SSD_HEADS // SSD_GROUPS
SSD_GROUP_WIDTH = SSD_HEADS_PER_GROUP * SSD_HEAD_DIM
SSD_COL_BLOCK = 2048
CONV_HALO = 8
PROJ_SUB_ROWS = 256

XA_HEADS = 4
XA_HEAD_DIM = 128

LANES = 128
VMEM_LIMIT_BYTES = 56 * 1024 * 1024


def _params(*sem):
    return pltpu.CompilerParams(dimension_semantics=sem, vmem_limit_bytes=VMEM_LIMIT_BYTES)


def _rms(x, g):
    return x * lax.rsqrt(jnp.mean(x * x, axis=-1, keepdims=True) + EPS) * g


def _silu(x):
    half = 0.5 * x
    return half + half * jnp.tanh(half)


def _dot(a, b):
    return jnp.dot(a, b, preferred_element_type=F32)


def _dot_nt(a, b):
    return lax.dot_general(a, b, (((1,), (1,)), ((), ())), preferred_element_type=F32)


def _dot_tn(a, b):
    return lax.dot_general(a, b, (((0,), (0,)), ((), ())), preferred_element_type=F32)


def _ffn_kernel(x_ref, g_ref, wg_ref, wu_ref, wd_ref, o_ref, h_ref):
    j = pl.program_id(1)

    @pl.when(j == 0)
    def _():
        h_ref[...] = _rms(x_ref[...], g_ref[...]).astype(BF16)
        o_ref[...] = jnp.zeros_like(o_ref)

    h = h_ref[...]
    gate = _dot(h, wg_ref[...].astype(BF16))
    up = _dot(h, wu_ref[...].astype(BF16))
    a = (_silu(gate) * up).astype(BF16)
    o_ref[...] += _dot(a, wd_ref[...].astype(BF16))

    @pl.when(j == pl.num_programs(1) - 1)
    def _():
        o_ref[...] = x_ref[...] + 0.5 * o_ref[...]


def _ffn(x, g, wg, wu, wd, layer, *, tm=1024, tf=256):
    s, d = x.shape
    f = wg.shape[2]
    tm = min(tm, s)
    return pl.pallas_call(
        _ffn_kernel,
        grid=(s // tm, f // tf),
        in_specs=[
            pl.BlockSpec((tm, d), lambda i, j: (i, 0)),
            pl.BlockSpec((1, d), lambda i, j: (0, 0)),
            pl.BlockSpec((None, d, tf), lambda i, j: (layer, 0, j)),
            pl.BlockSpec((None, d, tf), lambda i, j: (layer, 0, j)),
            pl.BlockSpec((None, tf, d), lambda i, j: (layer, j, 0)),
        ],
        out_specs=pl.BlockSpec((tm, d), lambda i, j: (i, 0)),
        out_shape=jax.ShapeDtypeStruct((s, d), F32),
        scratch_shapes=[pltpu.VMEM((tm, d), BF16)],
        compiler_params=_params("parallel", "arbitrary"),
        name="ffn",
    )(x, g.reshape(1, d), wg, wu, wd)


def _norm_kernel(x_ref, g_ref, o_ref):
    o_ref[...] = _rms(x_ref[...], g_ref[...]).astype(o_ref.dtype)


def _norm(x, g, *, tm=1024):
    s, d = x.shape
    tm = min(tm, s)
    return pl.pallas_call(
        _norm_kernel,
        grid=(s // tm,),
        in_specs=[pl.BlockSpec((tm, d), lambda i: (i, 0)), pl.BlockSpec((1, d), lambda i: (0, 0))],
        out_specs=pl.BlockSpec((tm, d), lambda i: (i, 0)),
        out_shape=jax.ShapeDtypeStruct((s, d), BF16),
        compiler_params=_params("parallel"),
        name="pre_norm",
    )(x, g.reshape(1, d))


def _proj_kernel(*refs, mode, n_valid):
    if mode == "qk_norm":
        h_ref, w_ref, cg_ref, o_ref, wb_ref = refs
    elif mode == "conv":
        h_ref, w_ref, cw_ref, cb_ref, o_ref, wb_ref, halo_ref = refs
    else:
        h_ref, w_ref, o_ref, wb_ref = refs
    j = pl.program_id(0)
    i = pl.program_id(1)

    @pl.when(i == 0)
    def _():
        w = w_ref[...]
        if n_valid < w.shape[1]:
            limit = jnp.where(j == pl.num_programs(0) - 1, n_valid, w.shape[1])
            w = jnp.where(lax.broadcasted_iota(jnp.int32, w.shape, 1) < limit, w, 0.0)
        wb_ref[...] = w.astype(BF16)
        if mode == "conv":
            halo_ref[...] = jnp.zeros(halo_ref.shape, F32)

    tm = h_ref.shape[0]
    sub = min(PROJ_SUB_ROWS, tm)
    halo = halo_ref[...] if mode == "conv" else None
    for r in range(tm // sub):
        rows = slice(r * sub, (r + 1) * sub)
        y = _dot(h_ref[rows, :], wb_ref[...])
        if mode == "qk_norm":
            for c in range(y.shape[1] // LANES):
                cols = slice(c * LANES, (c + 1) * LANES)
                o_ref[rows, cols] = _rms(y[:, cols], cg_ref[:, cols]).astype(o_ref.dtype)
        elif mode == "silu":
            o_ref[rows, :] = _silu(y).astype(o_ref.dtype)
        elif mode == "conv":
            ext = jnp.concatenate([halo, y], axis=0)
            acc = cb_ref[...] + cw_ref[SSD_CONV - 1:SSD_CONV, :] * y
            for k in range(SSD_CONV - 1):
                shifted = pltpu.roll(ext, SSD_CONV - 1 - k, axis=0)[CONV_HALO:, :]
                acc = acc + cw_ref[k:k + 1, :] * shifted
            halo = y[sub - CONV_HALO:, :]
            o_ref[rows, :] = _silu(acc).astype(o_ref.dtype)
        else:
            o_ref[rows, :] = y.astype(o_ref.dtype)
    if mode == "conv":
        halo_ref[...] = halo


def _proj(h, w, layer, *, mode, n_cols, first_col=0, extra=(), out_dtype, tm=1024, tn=1024):
    s, d = h.shape
    tm = min(tm, s)
    tn = min(tn, n_cols)
    assert n_cols % tn == 0 and first_col % tn == 0
    col0 = first_col // tn
    n_valid = min(tn, w.shape[2] - first_col - (n_cols - tn))
    in_specs = [
        pl.BlockSpec((tm, d), lambda j, i: (i, 0)),
        pl.BlockSpec((None, d, tn), lambda j, i: (layer, 0, col0 + j)),
    ]
    in_specs += [pl.BlockSpec((e.shape[0], tn), lambda j, i: (0, j)) for e in extra]
    scratch = [pltpu.VMEM((d, tn), BF16)]
    if mode == "conv":
        scratch.append(pltpu.VMEM((CONV_HALO, tn), F32))
    return pl.pallas_call(
        functools.partial(_proj_kernel, mode=mode, n_valid=n_valid),
        grid=(n_cols // tn, s // tm),
        in_specs=in_specs,
        out_specs=pl.BlockSpec((tm, tn), lambda j, i: (i, j)),
        out_shape=jax.ShapeDtypeStruct((s, n_cols), out_dtype),
        scratch_shapes=scratch,
        compiler_params=_params("arbitrary", "arbitrary"),
        name="proj_" + mode,
    )(h, w, *extra)


def _out_proj_kernel(a_ref, w_ref, x_ref, o_ref):
    o_ref[...] = x_ref[...] + _dot(a_ref[...], w_ref[...].astype(BF16))


def _out_proj(a, w, layer, x, *, tm=1024, tn=512):
    s, k = a.shape
    d = w.shape[2]
    tm = min(tm, s)
    return pl.pallas_call(
        _out_proj_kernel,
        grid=(s // tm, d // tn),
        in_specs=[
            pl.BlockSpec((tm, k), lambda i, j: (i, 0)),
            pl.BlockSpec((None, k, tn), lambda i, j: (layer, 0, j)),
            pl.BlockSpec((tm, tn), lambda i, j: (i, j)),
        ],
        out_specs=pl.BlockSpec((tm, tn), lambda i, j: (i, j)),
        out_shape=jax.ShapeDtypeStruct((s, d), F32),
        compiler_params=_params("parallel", "arbitrary"),
        name="out_proj",
    )(a, w, x)


def _bias_kernel(tab_ref, o_ref, *, t):
    h = pl.program_id(0)
    row = lax.broadcasted_iota(jnp.int32, (t, t), 0)
    col = lax.broadcasted_iota(jnp.int32, (t, t), 1)
    max_exact = REL_BUCKETS // 2
    far = tab_ref[REL_BUCKETS - 1, h]
    for blk in range(2):
        dist = row - col + blk * t
        n = jnp.maximum(dist, 0)
        nf = jnp.maximum(n, 1).astype(F32)
        large = max_exact + (jnp.log(nf / max_exact) / math.log(REL_MAX_DIST / max_exact)
                             * (REL_BUCKETS - max_exact)).astype(jnp.int32)
        large = jnp.minimum(large, REL_BUCKETS - 1)
        bucket = jnp.where(n < max_exact, n, large)
        bias = jnp.zeros((t, t), F32)
        for b in range(REL_BUCKETS - 1):
            bias = jnp.where(bucket == b, (tab_ref[b, h] - far) * LOG2E, bias)
        if blk == 0:
            bias = jnp.where(dist >= 0, bias, NEG_INF)
        o_ref[0, :, (1 - blk) * t:(2 - blk) * t] = bias


def _bias_tiles(rel_bias, t):
    assert t >= REL_MAX_DIST
    return pl.pallas_call(
        functools.partial(_bias_kernel, t=t),
        grid=(DA_HEADS,),
        in_specs=[pl.BlockSpec(memory_space=pltpu.SMEM)],
        out_specs=pl.BlockSpec((1, t, 2 * t), lambda h: (h, 0, 0)),
        out_shape=jax.ShapeDtypeStruct((DA_HEADS, t, 2 * t), F32),
        compiler_params=_params("arbitrary"),
        name="rel_bias_tiles",
    )(rel_bias)


def _attn_kernel(q_ref, k_ref, v_ref, bias_ref, lv_ref, sg_ref, o_ref, m_ref, l_ref, acc_ref, *, t, lambda_init,
                 online):
    qb = pl.program_id(1)
    dh = DA_HEAD_DIM
    if online:
        m_ref[...] = jnp.full(m_ref.shape, NEG_INF, F32)
    l_ref[...] = jnp.zeros(l_ref.shape, F32)
    acc_ref[...] = jnp.zeros(acc_ref.shape, F32)

    def lane_chunk_sum(p):
        out = p[:, 0:LANES]
        for c in range(1, p.shape[1] // LANES):
            out = out + p[:, c * LANES:(c + 1) * LANES]
        return out

    def block(start, width, bias):
        v = v_ref[pl.ds(start, width), :]
        for mp in range(2):
            cols = slice(mp * dh, (mp + 1) * dh)
            s = _dot_nt(q_ref[:, cols], k_ref[pl.ds(start, width), cols])
            if bias is not None:
                s = s + bias
            if online:
                m_prev = m_ref[mp]
                m_new = jnp.maximum(m_prev, jnp.max(s, axis=-1, keepdims=True))
                alpha = jnp.exp2(m_prev - m_new)
                p = jnp.exp2(s - m_new)
                l_ref[mp] = alpha * l_ref[mp] + lane_chunk_sum(p)
                acc_ref[mp] = alpha * acc_ref[mp] + _dot(p.astype(BF16), v)
                m_ref[mp] = m_new
            else:
                p = jnp.exp2(s)
                l_ref[mp] += lane_chunk_sum(p)
                acc_ref[mp] += _dot(p.astype(BF16), v)

    n_far = jnp.maximum(qb - 1, 0)
    n_groups = n_far // FAR_GROUP

    def far_group(i, carry):
        block(pl.multiple_of(i * (FAR_GROUP * t), FAR_GROUP * t), FAR_GROUP * t, None)
        return carry

    def far_single(i, carry):
        block(pl.multiple_of((n_groups * FAR_GROUP + i) * t, t), t, None)
        return carry

    lax.fori_loop(0, n_groups, far_group, 0)
    lax.fori_loop(0, n_far - n_groups * FAR_GROUP, far_single, 0)

    @pl.when(qb > 0)
    def _():
        block(pl.multiple_of((qb - 1) * t, t), 2 * t, bias_ref[0])

    @pl.when(qb == 0)
    def _():
        block(0, t, bias_ref[0, :, t:2 * t])

    lv = lv_ref[...]
    lam = (jnp.exp(jnp.sum(lv[0:1] * lv[1:2], axis=-1, keepdims=True))
           - jnp.exp(jnp.sum(lv[2:3] * lv[3:4], axis=-1, keepdims=True)) + lambda_init)
    l1 = jnp.sum(l_ref[0], axis=-1, keepdims=True)
    l2 = jnp.sum(l_ref[1], axis=-1, keepdims=True)
    o = acc_ref[0] / l1 - lam * (acc_ref[1] / l2)
    o_ref[...] = (_rms(o, sg_ref[...]) * (1.0 - lambda_init)).astype(o_ref.dtype)


def _diff_attention(qk, v, bias, lam_vecs, subln_gain, lambda_init, logit_bound, *, t):
    s = qk.shape[0]
    t = min(t, s)
    assert s >= max(2, FAR_GROUP) * t
    hw = DA_V_DIM

    def call(online):
        return pl.pallas_call(
            functools.partial(_attn_kernel, t=t, lambda_init=lambda_init, online=online),
            grid=(DA_HEADS, s // t),
            in_specs=[
                pl.BlockSpec((t, hw), lambda h, i: (i, h)),
                pl.BlockSpec((s, hw), lambda h, i: (0, DA_HEADS + h)),
                pl.BlockSpec((s, hw), lambda h, i: (0, h)),
                pl.BlockSpec((1, t, 2 * t), lambda h, i: (h, 0, 0)),
                pl.BlockSpec((4, DA_HEAD_DIM), lambda h, i: (0, 0)),
                pl.BlockSpec((1, hw), lambda h, i: (0, 0)),
            ],
            out_specs=pl.BlockSpec((t, hw), lambda h, i: (i, h)),
            out_shape=jax.ShapeDtypeStruct((s, DA_HEADS * hw), BF16),
            scratch_shapes=[
                pltpu.VMEM((2, t, 1), F32),
                pltpu.VMEM((2, t, LANES), F32),
                pltpu.VMEM((2, t, hw), F32),
            ],
            compiler_params=_params("parallel", "arbitrary"),
            name="diff_attn_online" if online else "diff_attn",
        )

    operands = (qk, qk, v, bias, lam_vecs, subln_gain.reshape(1, hw))
    return lax.cond(logit_bound < MAX_UNSHIFTED_LOG2_LOGIT,
                    lambda *a: call(False)(*a), lambda *a: call(True)(*a), *operands)


def _ssd_kernel(z0_ref, z1_ref, x0_ref, x1_ref, bc_ref, dt_ref, dtb_ref, alog_ref, dskip_ref,
                ng_ref, o_ref, bcs_ref, y_ref, state_ref):
    c = pl.program_id(0)
    ln = SSD_CHUNK
    cbw = SSD_COL_BLOCK
    ns = SSD_STATE

    @pl.when(c == 0)
    def _():
        state_ref[...] = jnp.zeros(state_ref.shape, F32)

    bcs_ref[...] = bc_ref[...].astype(BF16)
    x_refs = (x0_ref, x1_ref)

    dt = jax.nn.softplus(dt_ref[:, 0:SSD_HEADS] + dtb_ref[...])
    a = dt * (-jnp.exp(alog_ref[...]))
    row = lax.broadcasted_iota(jnp.int32, (ln, ln), 0)
    col = lax.broadcasted_iota(jnp.int32, (ln, ln), 1)
    causal = row >= col
    acum = jnp.dot(causal.astype(F32), a, preferred_element_type=F32, precision=lax.Precision.HIGHEST)
    acum_t = acum.T
    lane_lo = lax.broadcasted_iota(jnp.int32, (ln, LANES), 1) < SSD_HEAD_DIM

    for g in range(SSD_GROUPS):
        b_g = bcs_ref[:, g * ns:(g + 1) * ns]
        c_g = bcs_ref[:, SSD_GROUPS * ns + g * ns:SSD_GROUPS * ns + (g + 1) * ns]
        cb = _dot_nt(c_g, b_g)
        y_state = _dot(c_g, state_ref[g].astype(BF16))
        xw_parts = []
        decay_parts = []
        for pr in range(SSD_HEADS_PER_GROUP // 2):
            h0 = g * SSD_HEADS_PER_GROUP + 2 * pr
            ch = slice(h0 * SSD_HEAD_DIM, h0 * SSD_HEAD_DIM + LANES)
            m_parts = []
            a_cols = []
            for h in (h0, h0 + 1):
                a_col = jnp.broadcast_to(acum[:, h:h + 1], (ln, LANES))
                seg = a_col - acum_t[h:h + 1, :]
                m_parts.append((cb * jnp.where(causal, jnp.exp(seg), 0.0)).astype(BF16))
                a_cols.append(a_col)
            a_pair = jnp.where(lane_lo, a_cols[0], a_cols[1])
            dt_pair = jnp.where(lane_lo, dt[:, h0:h0 + 1], dt[:, h0 + 1:h0 + 2])
            x_pair = x_refs[ch.start // cbw][:, ch.start % cbw:ch.start % cbw + LANES]
            xdt = x_pair * dt_pair
            xdt_b = xdt.astype(BF16)
            zero = jnp.zeros_like(xdt_b)
            rhs = jnp.concatenate([jnp.where(lane_lo, xdt_b, zero), jnp.where(lane_lo, zero, xdt_b)], axis=0)
            y = _dot(jnp.concatenate(m_parts, axis=1), rhs)
            y = y + y_state[:, pr * LANES:(pr + 1) * LANES] * jnp.exp(a_pair)
            y_ref[:, ch] = y + x_pair * dskip_ref[:, ch]
            a_last = a_pair[ln - 1:ln, :]
            xw_parts.append((xdt * jnp.exp(a_last - a_pair)).astype(BF16))
            decay_parts.append(jnp.exp(a_last))
        xw = jnp.concatenate(xw_parts, axis=1)
        decay = jnp.concatenate(decay_parts, axis=1)
        state_ref[g] = state_ref[g] * decay + _dot_tn(b_g, xw)

    gated = jnp.concatenate([y_ref[:, 0:cbw] * z0_ref[...], y_ref[:, cbw:2 * cbw] * z1_ref[...]], axis=1)
    o_ref[...] = _rms(gated, ng_ref[...]).astype(o_ref.dtype)


def _ssd_scan(z_act, xbc, dt_raw, dt_bias, a_log, d_skip, norm_g):
    s = z_act.shape[0]
    ln = SSD_CHUNK
    cbw = SSD_COL_BLOCK
    assert SSD_INNER == 2 * cbw and 2 * SSD_GROUPS * SSD_STATE == cbw

    def col_block(j):
        return pl.BlockSpec((ln, cbw), lambda c: (c, j))

    def whole(shape):
        return pl.BlockSpec(shape, lambda c: (0,) * len(shape))

    return pl.pallas_call(
        _ssd_kernel,
        grid=(s // ln,),
        in_specs=[
            col_block(0), col_block(1), col_block(0), col_block(1), col_block(2),
            pl.BlockSpec((ln, LANES), lambda c: (c, 0)),
            whole((1, SSD_HEADS)), whole((1, SSD_HEADS)),
            whole((1, SSD_INNER)), whole((1, SSD_INNER)),
        ],
        out_specs=pl.BlockSpec((ln, SSD_INNER), lambda c: (c, 0)),
        out_shape=jax.ShapeDtypeStruct((s, SSD_INNER), BF16),
        scratch_shapes=[
            pltpu.VMEM((ln, 2 * SSD_GROUPS * SSD_STATE), BF16),
            pltpu.VMEM((ln, SSD_INNER), F32),
            pltpu.VMEM((SSD_GROUPS, SSD_STATE, SSD_GROUP_WIDTH), F32),
        ],
        compiler_params=_params("arbitrary"),
        name="ssd_scan",
    )(z_act, z_act, xbc, xbc, xbc, dt_raw,
      dt_bias.reshape(1, SSD_HEADS), a_log.reshape(1, SSD_HEADS),
      jnp.repeat(d_skip, SSD_HEAD_DIM).reshape(1, SSD_INNER), norm_g.reshape(1, SSD_INNER))


def _head_rms(y, gain, n_heads):
    return [_rms(y[:, h * XA_HEAD_DIM:(h + 1) * XA_HEAD_DIM], gain) for h in range(n_heads)]


def _mem_kv_kernel(mem_ref, g_ref, w_ref, kg_ref, k_ref, v_ref):
    xw = XA_HEADS * XA_HEAD_DIM
    kv = _dot(_rms(mem_ref[...], g_ref[...]).astype(BF16), w_ref[...].astype(BF16))
    k_ref[...] = jnp.concatenate(_head_rms(kv[:, 0:xw], kg_ref[...], XA_HEADS), axis=1).astype(BF16)
    v_ref[...] = kv[:, xw:2 * xw].astype(BF16)


def _mem_kv(mem, g, w_kv, layer, k_gain):
    m, d = mem.shape
    xw = XA_HEADS * XA_HEAD_DIM

    def whole(shape):
        return pl.BlockSpec(shape, lambda i: (0,) * len(shape))

    return pl.pallas_call(
        _mem_kv_kernel,
        grid=(1,),
        in_specs=[whole((m, d)), whole((1, d)), pl.BlockSpec((None, d, 2 * xw), lambda i: (layer, 0, 0)),
                  whole((1, XA_HEAD_DIM))],
        out_specs=(whole((m, xw)), whole((m, xw))),
        out_shape=(jax.ShapeDtypeStruct((m, xw), BF16), jax.ShapeDtypeStruct((m, xw), BF16)),
        compiler_params=_params("arbitrary"),
        name="mem_kv",
    )(mem, g.reshape(1, d), w_kv, k_gain.reshape(1, XA_HEAD_DIM))


def _xattn_kernel(x_ref, g_ref, wq_ref, qg_ref, k_ref, v_ref, wo_ref, o_ref, wq_b_ref, wo_b_ref):
    @pl.when(pl.program_id(0) == 0)
    def _():
        wq_b_ref[...] = wq_ref[...].astype(BF16)
        wo_b_ref[...] = wo_ref[...].astype(BF16)

    x = x_ref[...]
    q = _dot(_rms(x, g_ref[...]).astype(BF16), wq_b_ref[...])
    heads = []
    for h, qh in enumerate(_head_rms(q, qg_ref[...], XA_HEADS)):
        cols = slice(h * XA_HEAD_DIM, (h + 1) * XA_HEAD_DIM)
        s = _dot_nt((qh * XA_HEAD_DIM ** -0.5).astype(BF16), k_ref[:, cols])
        p = jnp.exp(s - jnp.max(s, axis=-1, keepdims=True))
        p = p / jnp.sum(p, axis=-1, keepdims=True)
        heads.append(_dot(p.astype(BF16), v_ref[:, cols]))
    o = jnp.concatenate(heads, axis=1).astype(BF16)
    o_ref[...] = x + _dot(o, wo_b_ref[...])


def _xattn(x, g, w_q, q_gain, k, v, w_o, layer, *, tm=512):
    s, d = x.shape
    m, xw = k.shape
    tm = min(tm, s)

    def whole(shape):
        return pl.BlockSpec(shape, lambda i: (0,) * len(shape))

    return pl.pallas_call(
        _xattn_kernel,
        grid=(s // tm,),
        in_specs=[
            pl.BlockSpec((tm, d), lambda i: (i, 0)),
            whole((1, d)), pl.BlockSpec((None, d, xw), lambda i: (layer, 0, 0)), whole((1, XA_HEAD_DIM)),
            whole((m, xw)), whole((m, xw)), pl.BlockSpec((None, xw, d), lambda i: (layer, 0, 0)),
        ],
        out_specs=pl.BlockSpec((tm, d), lambda i: (i, 0)),
        out_shape=jax.ShapeDtypeStruct((s, d), F32),
        scratch_shapes=[pltpu.VMEM((d, xw), BF16), pltpu.VMEM((xw, d), BF16)],
        compiler_params=_params("arbitrary"),
        name="mem_xattn",
    )(x, g.reshape(1, d), w_q, q_gain.reshape(1, XA_HEAD_DIM), k, v, w_o)


def _attn_layer(x, norm_g, w_qkv, w_o, layer, q_gain, k_gain, lam_vecs, subln_gain, rel_bias, bias, lambda_init, t):
    d = x.shape[1]
    n_maps = 2 * DA_HEADS
    q_col_gain = q_gain * (DA_HEAD_DIM ** -0.5 * LOG2E)
    col_gain = jnp.concatenate([jnp.tile(q_col_gain, n_maps), jnp.tile(k_gain, n_maps)]).reshape(1, 2 * d)
    h = _norm(x, norm_g)
    qk = _proj(h, w_qkv, layer, mode="qk_norm", n_cols=2 * d, extra=(col_gain,), out_dtype=BF16)
    v = _proj(h, w_qkv, layer, mode="plain", n_cols=d, first_col=2 * d, out_dtype=BF16)
    logit_bound = (DA_HEAD_DIM * jnp.max(jnp.abs(q_col_gain)) * jnp.max(jnp.abs(k_gain))
                   + LOG2E * jnp.max(jnp.abs(rel_bias - rel_bias[REL_BUCKETS - 1])))
    o = _diff_attention(qk, v, bias, lam_vecs, subln_gain, lambda_init, logit_bound, t=t)
    return _out_proj(o, w_o, layer, x)


def _ssd_layer(x, norm_g, w_in, w_out, layer, conv_w, conv_b, dt_bias, a_log, d_skip, ssd_norm):
    n_main = 2 * SSD_INNER + 2 * SSD_GROUPS * SSD_STATE
    h = _norm(x, norm_g)
    z_act = _proj(h, w_in, layer, mode="silu", n_cols=SSD_INNER, out_dtype=F32)
    xbc = _proj(h, w_in, layer, mode="conv", n_cols=n_main - SSD_INNER, first_col=SSD_INNER,
                extra=(conv_w, conv_b.reshape(1, -1)), out_dtype=F32)
    dt_raw = _proj(h, w_in, layer, mode="plain", n_cols=LANES, first_col=n_main, out_dtype=F32, tn=LANES)
    y = _ssd_scan(z_act, xbc, dt_raw, dt_bias, a_log, d_skip, ssd_norm)
    return _out_proj(y, w_out, layer, x)


@jax.jit
def kernel(x, mem, rel_bias, ffn1_norm, ffn1_w_gate, ffn1_w_up, ffn1_w_down, mix_norm, attn_w_qkv, attn_w_o,
           attn_q_norm, attn_k_norm, attn_lambda, attn_subln, ssd_w_in, ssd_conv_w, ssd_conv_b, ssd_dt_bias,
           ssd_a_log, ssd_d, ssd_norm, ssd_w_out, xattn_norm, mem_norm, xattn_w_q, xattn_w_kv, xattn_w_o,
           xattn_q_norm, xattn_k_norm, ffn2_norm, ffn2_w_gate, ffn2_w_up, ffn2_w_down):
    b, s, d = x.shape
    assert b == 1
    t = min(ATTN_BLOCK, s)
    xs = x[0]
    mem2 = mem[0]
    bias = _bias_tiles(rel_bias, t)
    for i in range(DEPTH):
        xs = _ffn(xs, ffn1_norm[i], ffn1_w_gate, ffn1_w_up, ffn1_w_down, i)
        j = i // N_MIXERS
        if i % N_MIXERS == 0:
            lambda_init = 0.8 - 0.6 * math.exp(-0.3 * i)
            xs = _attn_layer(xs, mix_norm[i], attn_w_qkv, attn_w_o, j, attn_q_norm[j], attn_k_norm[j],
                             attn_lambda[j], attn_subln[j], rel_bias, bias, lambda_init, t)
        else:
            xs = _ssd_layer(xs, mix_norm[i], ssd_w_in, ssd_w_out, j, ssd_conv_w[j], ssd_conv_b[j], ssd_dt_bias[j],
                            ssd_a_log[j], ssd_d[j], ssd_norm[j])
        k, v = _mem_kv(mem2, mem_norm[i], xattn_w_kv, i, xattn_k_norm[i])
        xs = _xattn(xs, xattn_norm[i], xattn_w_q, xattn_q_norm[i], k, v, xattn_w_o, i)
        xs = _ffn(xs, ffn2_norm[i], ffn2_w_gate, ffn2_w_up, ffn2_w_down, i)
    return xs[None]
```

```python
import functools
import math

import jax
import jax.numpy as jnp
from jax import lax
from jax.experimental import pallas as pl
from jax.experimental.pallas import tpu as pltpu

F32 = jnp.float32
BF16 = jnp.bfloat16

EPS = 1e-6
NEG_INF = -1e30
LOG2E = math.log2(math.e)

DEPTH = 4
N_MIXERS = 2

DA_HEADS = 8
DA_HEAD_DIM = 128
DA_V_DIM = 2 * DA_HEAD_DIM
REL_BUCKETS = 32
REL_MAX_DIST = 128
ATTN_BLOCK = 512
FAR_GROUP = 4
MAX_UNSHIFTED_LOG2_LOGIT = 100.0

SSD_HEAD_DIM = 64
SSD_HEADS = 64
SSD_GROUPS = 8
SSD_STATE = 128
SSD_CONV = 4
SSD_CHUNK = 128
SSD_INNER = SSD_HEADS * SSD_HEAD_DIM
SSD_HEADS_PER_GROUP = SSD_HEADS // SSD_GROUPS
SSD_GROUP_WIDTH = SSD_HEADS_PER_GROUP * SSD_HEAD_DIM
SSD_COL_BLOCK = 2048
CONV_HALO = 8
PROJ_SUB_ROWS = 256

XA_HEADS = 4
XA_HEAD_DIM = 128

LANES = 128
VMEM_LIMIT_BYTES = 56 * 1024 * 1024


def _params(*sem):
    return pltpu.CompilerParams(dimension_semantics=sem, vmem_limit_bytes=VMEM_LIMIT_BYTES)


def _rms(x, g):
    return x * lax.rsqrt(jnp.mean(x * x, axis=-1, keepdims=True) + EPS) * g


def _silu(x):
    half = 0.5 * x
    return half + half * jnp.tanh(half)


def _dot(a, b):
    return jnp.dot(a, b, preferred_element_type=F32)


def _dot_nt(a, b):
    return lax.dot_general(a, b, (((1,), (1,)), ((), ())), preferred_element_type=F32)


def _dot_tn(a, b):
    return lax.dot_general(a, b, (((0,), (0,)), ((), ())), preferred_element_type=F32)


def _ffn_kernel(x_ref, g_ref, wg_ref, wu_ref, wd_ref, o_ref, h_ref):
    j = pl.program_id(1)

    @pl.when(j == 0)
    def _():
        h_ref[...] = _rms(x_ref[...], g_ref[...]).astype(BF16)
        o_ref[...] = jnp.zeros_like(o_ref)

    h = h_ref[...]
    gate = _dot(h, wg_ref[...].astype(BF16))
    up = _dot(h, wu_ref[...].astype(BF16))
    a = (_silu(gate) * up).astype(BF16)
    o_ref[...] += _dot(a, wd_ref[...].astype(BF16))

    @pl.when(j == pl.num_programs(1) - 1)
    def _():
        o_ref[...] = x_ref[...] + 0.5 * o_ref[...]


def _ffn(x, g, wg, wu, wd, layer, *, tm=1024, tf=256):
    s, d = x.shape
    f = wg.shape[2]
    tm = min(tm, s)
    return pl.pallas_call(
        _ffn_kernel,
        grid=(s // tm, f // tf),
        in_specs=[
            pl.BlockSpec((tm, d), lambda i, j: (i, 0)),
            pl.BlockSpec((1, d), lambda i, j: (0, 0)),
            pl.BlockSpec((None, d, tf), lambda i, j: (layer, 0, j)),
            pl.BlockSpec((None, d, tf), lambda i, j: (layer, 0, j)),
            pl.BlockSpec((None, tf, d), lambda i, j: (layer, j, 0)),
        ],
        out_specs=pl.BlockSpec((tm, d), lambda i, j: (i, 0)),
        out_shape=jax.ShapeDtypeStruct((s, d), F32),
        scratch_shapes=[pltpu.VMEM((tm, d), BF16)],
        compiler_params=_params("parallel", "arbitrary"),
        name="ffn",
    )(x, g.reshape(1, d), wg, wu, wd)


def _norm_kernel(x_ref, g_ref, o_ref):
    o_ref[...] = _rms(x_ref[...], g_ref[...]).astype(o_ref.dtype)


def _norm(x, g, *, tm=1024):
    s, d = x.shape
    tm = min(tm, s)
    return pl.pallas_call(
        _norm_kernel,
        grid=(s // tm,),
        in_specs=[pl.BlockSpec((tm, d), lambda i: (i, 0)), pl.BlockSpec((1, d), lambda i: (0, 0))],
        out_specs=pl.BlockSpec((tm, d), lambda i: (i, 0)),
        out_shape=jax.ShapeDtypeStruct((s, d), BF16),
        compiler_params=_params("parallel"),
        name="pre_norm",
    )(x, g.reshape(1, d))


def _proj_kernel(*refs, mode, n_valid, transposed):
    if mode == "qk_norm":
        h_ref, w_ref, cg_ref, o_ref, wb_ref = refs
    elif mode == "residual":
        h_ref, w_ref, x_ref, o_ref, wb_ref = refs
    elif mode == "conv":
        h_ref, w_ref, cw_ref, cb_ref, o_ref, wb_ref, halo_ref = refs
    else:
        h_ref, w_ref, o_ref, wb_ref = refs
    j = pl.program_id(0)
    i = pl.program_id(1)

    @pl.when(i == 0)
    def _():
        w = w_ref[...]
        col_axis = 0 if transposed else 1
        if n_valid < w.shape[col_axis]:
            limit = jnp.where(j == pl.num_programs(0) - 1, n_valid, w.shape[col_axis])
            w = jnp.where(lax.broadcasted_iota(jnp.int32, w.shape, col_axis) < limit, w, 0.0)
        wb_ref[...] = w.astype(BF16)
        if mode == "conv":
            halo_ref[...] = jnp.zeros(halo_ref.shape, F32)

    tm = h_ref.shape[0]
    sub = min(PROJ_SUB_ROWS, tm)
    halo = halo_ref[...] if mode == "conv" else None
    for r in range(tm // sub):
        rows = slice(r * sub, (r + 1) * sub)
        y = (_dot_nt if transposed else _dot)(h_ref[rows, :], wb_ref[...])
        if mode == "qk_norm":
            for c in range(y.shape[1] // LANES):
                cols = slice(c * LANES, (c + 1) * LANES)
                o_ref[rows, cols] = _rms(y[:, cols], cg_ref[:, cols]).astype(o_ref.dtype)
        elif mode == "silu":
            o_ref[rows, :] = _silu(y).astype(o_ref.dtype)
        elif mode == "residual":
            o_ref[rows, :] = x_ref[rows, :] + y
        elif mode == "conv":
            ext = jnp.concatenate([halo, y], axis=0)
            acc = cb_ref[...] + cw_ref[SSD_CONV - 1:SSD_CONV, :] * y
            for k in range(SSD_CONV - 1):
                shifted = pltpu.roll(ext, SSD_CONV - 1 - k, axis=0)[CONV_HALO:, :]
                acc = acc + cw_ref[k:k + 1, :] * shifted
            halo = y[sub - CONV_HALO:, :]
            o_ref[rows, :] = _silu(acc).astype(o_ref.dtype)
        else:
            o_ref[rows, :] = y.astype(o_ref.dtype)
    if mode == "conv":
        halo_ref[...] = halo


def _proj(h, w, layer, *, mode, n_cols, first_col=0, extra=(), out_dtype, transposed=False, tm=1024, tn=1024):
    s, d = h.shape
    tm = min(tm, s)
    tn = min(tn, n_cols)
    assert n_cols % tn == 0 and first_col % tn == 0
    col0 = first_col // tn
    w_cols = w.shape[1] if transposed else w.shape[2]
    n_valid = min(tn, w_cols - first_col - (n_cols - tn))
    if transposed:
        w_spec = pl.BlockSpec((None, tn, d), lambda j, i: (layer, col0 + j, 0))
    else:
        w_spec = pl.BlockSpec((None, d, tn), lambda j, i: (layer, 0, col0 + j))
    in_specs = [pl.BlockSpec((tm, d), lambda j, i: (i, 0)), w_spec]
    if mode == "residual":
        in_specs.append(pl.BlockSpec((tm, tn), lambda j, i: (i, j)))
    else:
        in_specs += [pl.BlockSpec((e.shape[0], tn), lambda j, i: (0, j)) for e in extra]
    scratch = [pltpu.VMEM((tn, d) if transposed else (d, tn), BF16)]
    if mode == "conv":
        scratch.append(pltpu.VMEM((CONV_HALO, tn), F32))
    return pl.pallas_call(
        functools.partial(_proj_kernel, mode=mode, n_valid=n_valid, transposed=transposed),
        grid=(n_cols // tn, s // tm),
        in_specs=in_specs,
        out_specs=pl.BlockSpec((tm, tn), lambda j, i: (i, j)),
        out_shape=jax.ShapeDtypeStruct((s, n_cols), out_dtype),
        scratch_shapes=scratch,
        compiler_params=_params("arbitrary", "arbitrary"),
        name="proj_" + mode,
    )(h, w, *extra)


def _bias_kernel(tab_ref, o_ref, *, t):
    h = pl.program_id(0)
    row = lax.broadcasted_iota(jnp.int32, (t, t), 0)
    col = lax.broadcasted_iota(jnp.int32, (t, t), 1)
    max_exact = REL_BUCKETS // 2
    far = tab_ref[REL_BUCKETS - 1, h]
    for blk in range(2):
        dist = row - col + blk * t
        n = jnp.maximum(dist, 0)
        nf = jnp.maximum(n, 1).astype(F32)
        large = max_exact + (jnp.log(nf / max_exact) / math.log(REL_MAX_DIST / max_exact)
                             * (REL_BUCKETS - max_exact)).astype(jnp.int32)
        large = jnp.minimum(large, REL_BUCKETS - 1)
        bucket = jnp.where(n < max_exact, n, large)
        bias = jnp.zeros((t, t), F32)
        for b in range(REL_BUCKETS - 1):
            bias = jnp.where(bucket == b, (tab_ref[b, h] - far) * LOG2E, bias)
        if blk == 0:
            bias = jnp.where(dist >= 0, bias, NEG_INF)
        o_ref[0, :, (1 - blk) * t:(2 - blk) * t] = bias


def _bias_tiles(rel_bias, t):
    assert t >= REL_MAX_DIST
    return pl.pallas_call(
        functools.partial(_bias_kernel, t=t),
        grid=(DA_HEADS,),
        in_specs=[pl.BlockSpec(memory_space=pltpu.SMEM)],
        out_specs=pl.BlockSpec((1, t, 2 * t), lambda h: (h, 0, 0)),
        out_shape=jax.ShapeDtypeStruct((DA_HEADS, t, 2 * t), F32),
        compiler_params=_params("arbitrary"),
        name="rel_bias_tiles",
    )(rel_bias)


def _attn_kernel(q_ref, k_ref, v_ref, bias_ref, lv_ref, sg_ref, o_ref, m_ref, l_ref, acc_ref, *, t, lambda_init,
                 online):
    qb = pl.program_id(1)
    dh = DA_HEAD_DIM
    if online:
        m_ref[...] = jnp.full(m_ref.shape, NEG_INF, F32)
    l_ref[...] = jnp.zeros(l_ref.shape, F32)
    acc_ref[...] = jnp.zeros(acc_ref.shape, F32)

    def lane_chunk_sum(p):
        out = p[:, 0:LANES]
        for c in range(1, p.shape[1] // LANES):
            out = out + p[:, c * LANES:(c + 1) * LANES]
        return out

    def block(start, width, bias):
        v = v_ref[pl.ds(start, width), :]
        for mp in range(2):
            cols = slice(mp * dh, (mp + 1) * dh)
            s = _dot_nt(q_ref[:, cols], k_ref[pl.ds(start, width), cols])
            if bias is not None:
                s = s + bias
            if online:
                m_prev = m_ref[mp]
                m_new = jnp.maximum(m_prev, jnp.max(s, axis=-1, keepdims=True))
                alpha = jnp.exp2(m_prev - m_new)
                p = jnp.exp2(s - m_new)
                l_ref[mp] = alpha * l_ref[mp] + lane_chunk_sum(p)
                acc_ref[mp] = alpha * acc_ref[mp] + _dot(p.astype(BF16), v)
                m_ref[mp] = m_new
            else:
                p = jnp.exp2(s)
                l_ref[mp] += lane_chunk_sum(p)
                acc_ref[mp] += _dot(p.astype(BF16), v)

    n_far = jnp.maximum(qb - 1, 0)
    n_groups = n_far // FAR_GROUP

    def far_group(i, carry):
        block(pl.multiple_of(i * (FAR_GROUP * t), FAR_GROUP * t), FAR_GROUP * t, None)
        return carry

    def far_single(i, carry):
        block(pl.multiple_of((n_groups * FAR_GROUP + i) * t, t), t, None)
        return carry

    lax.fori_loop(0, n_groups, far_group, 0)
    lax.fori_loop(0, n_far - n_groups * FAR_GROUP, far_single, 0)

    @pl.when(qb > 0)
    def _():
        block(pl.multiple_of((qb - 1) * t, t), 2 * t, bias_ref[0])

    @pl.when(qb == 0)
    def _():
        block(0, t, bias_ref[0, :, t:2 * t])

    lv = lv_ref[...]
    lam = (jnp.exp(jnp.sum(lv[0:1] * lv[1:2], axis=-1, keepdims=True))
           - jnp.exp(jnp.sum(lv[2:3] * lv[3:4], axis=-1, keepdims=True)) + lambda_init)
    l1 = jnp.sum(l_ref[0], axis=-1, keepdims=True)
    l2 = jnp.sum(l_ref[1], axis=-1, keepdims=True)
    o = acc_ref[0] / l1 - lam * (acc_ref[1] / l2)
    o_ref[...] = (_rms(o, sg_ref[...]) * (1.0 - lambda_init)).astype(o_ref.dtype)


def _diff_attention(qk, v, bias, lam_vecs, subln_gain, lambda_init, logit_bound, *, t):
    s = qk.shape[0]
    t = min(t, s)
    assert s >= max(2, FAR_GROUP) * t
    hw = DA_V_DIM

    def call(online):
        return pl.pallas_call(
            functools.partial(_attn_kernel, t=t, lambda_init=lambda_init, online=online),
            grid=(DA_HEADS, s // t),
            in_specs=[
                pl.BlockSpec((t, hw), lambda h, i: (i, h)),
                pl.BlockSpec((s, hw), lambda h, i: (0, DA_HEADS + h)),
                pl.BlockSpec((s, hw), lambda h, i: (0, h)),
                pl.BlockSpec((1, t, 2 * t), lambda h, i: (h, 0, 0)),
                pl.BlockSpec((4, DA_HEAD_DIM), lambda h, i: (0, 0)),
                pl.BlockSpec((1, hw), lambda h, i: (0, 0)),
            ],
            out_specs=pl.BlockSpec((t, hw), lambda h, i: (i, h)),
            out_shape=jax.ShapeDtypeStruct((s, DA_HEADS * hw), BF16),
            scratch_shapes=[
                pltpu.VMEM((2, t, 1), F32),
                pltpu.VMEM((2, t, LANES), F32),
                pltpu.VMEM((2, t, hw), F32),
            ],
            compiler_params=_params("parallel", "arbitrary"),
            name="diff_attn_online" if online else "diff_attn",
        )

    operands = (qk, qk, v, bias, lam_vecs, subln_gain.reshape(1, hw))
    return lax.cond(logit_bound < MAX_UNSHIFTED_LOG2_LOGIT,
                    lambda *a: call(False)(*a), lambda *a: call(True)(*a), *operands)


def _ssd_kernel(z0_ref, z1_ref, x0_ref, x1_ref, bc_ref, dt_ref, dtb_ref, alog_ref, dskip_ref,
                ng_ref, o_ref, bcs_ref, y_ref, state_ref):
    c = pl.program_id(0)
    ln = SSD_CHUNK
    cbw = SSD_COL_BLOCK
    ns = SSD_STATE

    @pl.when(c == 0)
    def _():
        state_ref[...] = jnp.zeros(state_ref.shape, F32)

    bcs_ref[...] = bc_ref[...].astype(BF16)
    x_refs = (x0_ref, x1_ref)

    dt = jax.nn.softplus(dt_ref[:, 0:SSD_HEADS] + dtb_ref[...])
    a = dt * (-LOG2E * jnp.exp(alog_ref[...]))
    row = lax.broadcasted_iota(jnp.int32, (ln, ln), 0)
    col = lax.broadcasted_iota(jnp.int32, (ln, ln), 1)
    causal = row >= col
    acum = jnp.dot(causal.astype(F32), a, preferred_element_type=F32, precision=lax.Precision.HIGHEST)
    acum_t = acum.T
    lane_lo = lax.broadcasted_iota(jnp.int32, (ln, LANES), 1) < SSD_HEAD_DIM

    for g in range(SSD_GROUPS):
        b_g = bcs_ref[:, g * ns:(g + 1) * ns]
        c_g = bcs_ref[:, SSD_GROUPS * ns + g * ns:SSD_GROUPS * ns + (g + 1) * ns]
        cb = _dot_nt(c_g, b_g)
        y_state = _dot(c_g, state_ref[g].astype(BF16))
        xw_parts = []
        decay_parts = []
        for pr in range(SSD_HEADS_PER_GROUP // 2):
            h0 = g * SSD_HEADS_PER_GROUP + 2 * pr
            ch = slice(h0 * SSD_HEAD_DIM, h0 * SSD_HEAD_DIM + LANES)
            m_parts = []
            a_cols = []
            for h in (h0, h0 + 1):
                a_col = jnp.broadcast_to(acum[:, h:h + 1], (ln, LANES))
                seg = a_col - acum_t[h:h + 1, :]
                m_parts.append((cb * jnp.where(causal, jnp.exp2(seg), 0.0)).astype(BF16))
                a_cols.append(a_col)
            a_pair = jnp.where(lane_lo, a_cols[0], a_cols[1])
            dt_pair = jnp.where(lane_lo, dt[:, h0:h0 + 1], dt[:, h0 + 1:h0 + 2])
            x_pair = x_refs[ch.start // cbw][:, ch.start % cbw:ch.start % cbw + LANES]
            xdt = x_pair * dt_pair
            xdt_b = xdt.astype(BF16)
            zero = jnp.zeros_like(xdt_b)
            rhs = jnp.concatenate([jnp.where(lane_lo, xdt_b, zero), jnp.where(lane_lo, zero, xdt_b)], axis=0)
            y = _dot(jnp.concatenate(m_parts, axis=1), rhs)
            y = y + y_state[:, pr * LANES:(pr + 1) * LANES] * jnp.exp2(a_pair)
            y_ref[:, ch] = y + x_pair * dskip_ref[:, ch]
            a_last = a_pair[ln - 1:ln, :]
            xw_parts.append((xdt * jnp.exp2(a_last - a_pair)).astype(BF16))
            decay_parts.append(jnp.exp2(a_last))
        xw = jnp.concatenate(xw_parts, axis=1)
        decay = jnp.concatenate(decay_parts, axis=1)
        state_ref[g] = state_ref[g] * decay + _dot_tn(b_g, xw)

    gated = jnp.concatenate([y_ref[:, 0:cbw] * z0_ref[...], y_ref[:, cbw:2 * cbw] * z1_ref[...]], axis=1)
    o_ref[...] = _rms(gated, ng_ref[...]).astype(o_ref.dtype)


def _ssd_scan(z_act, xbc, dt_raw, dt_bias, a_log, d_skip, norm_g):
    s = z_act.shape[0]
    ln = SSD_CHUNK
    cbw = SSD_COL_BLOCK
    assert SSD_INNER == 2 * cbw and 2 * SSD_GROUPS * SSD_STATE == cbw

    def col_block(j):
        return pl.BlockSpec((ln, cbw), lambda c: (c, j))

    def whole(shape):
        return pl.BlockSpec(shape, lambda c: (0,) * len(shape))

    return pl.pallas_call(
        _ssd_kernel,
        grid=(s // ln,),
        in_specs=[
            col_block(0), col_block(1), col_block(0), col_block(1), col_block(2),
            pl.BlockSpec((ln, LANES), lambda c: (c, 0)),
            whole((1, SSD_HEADS)), whole((1, SSD_HEADS)),
            whole((1, SSD_INNER)), whole((1, SSD_INNER)),
        ],
        out_specs=pl.BlockSpec((ln, SSD_INNER), lambda c: (c, 0)),
        out_shape=jax.ShapeDtypeStruct((s, SSD_INNER), BF16),
        scratch_shapes=[
            pltpu.VMEM((ln, 2 * SSD_GROUPS * SSD_STATE), BF16),
            pltpu.VMEM((ln, SSD_INNER), F32),
            pltpu.VMEM((SSD_GROUPS, SSD_STATE, SSD_GROUP_WIDTH), F32),
        ],
        compiler_params=_params("arbitrary"),
        name="ssd_scan",
    )(z_act, z_act, xbc, xbc, xbc, dt_raw,
      dt_bias.reshape(1, SSD_HEADS), a_log.reshape(1, SSD_HEADS),
      jnp.repeat(d_skip, SSD_HEAD_DIM).reshape(1, SSD_INNER), norm_g.reshape(1, SSD_INNER))


def _head_rms(y, gain, n_heads):
    return [_rms(y[:, h * XA_HEAD_DIM:(h + 1) * XA_HEAD_DIM], gain) for h in range(n_heads)]


def _mem_kv_kernel(mem_ref, g_ref, w_ref, kg_ref, k_ref, v_ref):
    xw = XA_HEADS * XA_HEAD_DIM
    kv = _dot(_rms(mem_ref[...], g_ref[...]).astype(BF16), w_ref[...].astype(BF16))
    k_ref[...] = jnp.concatenate(_head_rms(kv[:, 0:xw], kg_ref[...], XA_HEADS), axis=1).astype(BF16)
    v_ref[...] = kv[:, xw:2 * xw].astype(BF16)


def _mem_kv(mem, g, w_kv, layer, k_gain):
    m, d = mem.shape
    xw = XA_HEADS * XA_HEAD_DIM

    def whole(shape):
        return pl.BlockSpec(shape, lambda i: (0,) * len(shape))

    return pl.pallas_call(
        _mem_kv_kernel,
        grid=(1,),
        in_specs=[whole((m, d)), whole((1, d)), pl.BlockSpec((None, d, 2 * xw), lambda i: (layer, 0, 0)),
                  whole((1, XA_HEAD_DIM))],
        out_specs=(whole((m, xw)), whole((m, xw))),
        out_shape=(jax.ShapeDtypeStruct((m, xw), BF16), jax.ShapeDtypeStruct((m, xw), BF16)),
        compiler_params=_params("arbitrary"),
        name="mem_kv",
    )(mem, g.reshape(1, d), w_kv, k_gain.reshape(1, XA_HEAD_DIM))


def _xattn_kernel(x_ref, g_ref, wq_ref, qg_ref, k_ref, v_ref, wo_ref, o_ref, wq_b_ref, wo_b_ref):
    @pl.when(pl.program_id(0) == 0)
    def _():
        wq_b_ref[...] = wq_ref[...].astype(BF16)
        wo_b_ref[...] = wo_ref[...].astype(BF16)

    x = x_ref[...]
    q = _dot(_rms(x, g_ref[...]).astype(BF16), wq_b_ref[...])
    heads = []
    for h, qh in enumerate(_head_rms(q, qg_ref[...], XA_HEADS)):
        cols = slice(h * XA_HEAD_DIM, (h + 1) * XA_HEAD_DIM)
        s = _dot_nt((qh * XA_HEAD_DIM ** -0.5).astype(BF16), k_ref[:, cols])
        p = jnp.exp(s - jnp.max(s, axis=-1, keepdims=True))
        p = p / jnp.sum(p, axis=-1, keepdims=True)
        heads.append(_dot(p.astype(BF16), v_ref[:, cols]))
    o = jnp.concatenate(heads, axis=1).astype(BF16)
    o_ref[...] = x + _dot(o, wo_b_ref[...])


def _xattn(x, g, w_q, q_gain, k, v, w_o, layer, *, tm=512):
    s, d = x.shape
    m, xw = k.shape
    tm = min(tm, s)

    def whole(shape):
        return pl.BlockSpec(shape, lambda i: (0,) * len(shape))

    return pl.pallas_call(
        _xattn_kernel,
        grid=(s // tm,),
        in_specs=[
            pl.BlockSpec((tm, d), lambda i: (i, 0)),
            whole((1, d)), pl.BlockSpec((None, d, xw), lambda i: (layer, 0, 0)), whole((1, XA_HEAD_DIM)),
            whole((m, xw)), whole((m, xw)), pl.BlockSpec((None, xw, d), lambda i: (layer, 0, 0)),
        ],
        out_specs=pl.BlockSpec((tm, d), lambda i: (i, 0)),
        out_shape=jax.ShapeDtypeStruct((s, d), F32),
        scratch_shapes=[pltpu.VMEM((d, xw), BF16), pltpu.VMEM((xw, d), BF16)],
        compiler_params=_params("arbitrary"),
        name="mem_xattn",
    )(x, g.reshape(1, d), w_q, q_gain.reshape(1, XA_HEAD_DIM), k, v, w_o)


def _attn_layer(x, norm_g, w_qkv, w_o, layer, q_gain, k_gain, lam_vecs, subln_gain, rel_bias, bias, lambda_init, t):
    d = x.shape[1]
    n_maps = 2 * DA_HEADS
    q_col_gain = q_gain * (DA_HEAD_DIM ** -0.5 * LOG2E)
    col_gain = jnp.concatenate([jnp.tile(q_col_gain, n_maps), jnp.tile(k_gain, n_maps)]).reshape(1, 2 * d)
    h = _norm(x, norm_g)
    qk = _proj(h, w_qkv, layer, mode="qk_norm", n_cols=2 * d, extra=(col_gain,), out_dtype=BF16)
    v = _proj(h, w_qkv, layer, mode="plain", n_cols=d, first_col=2 * d, out_dtype=BF16)
    logit_bound = (DA_HEAD_DIM * jnp.max(jnp.abs(q_col_gain)) * jnp.max(jnp.abs(k_gain))
                   + LOG2E * jnp.max(jnp.abs(rel_bias - rel_bias[REL_BUCKETS - 1])))
    o = _diff_attention(qk, v, bias, lam_vecs, subln_gain, lambda_init, logit_bound, t=t)
    return _proj(o, w_o, layer, mode="residual", n_cols=d, extra=(x,), out_dtype=F32)


def _ssd_layer(x, norm_g, w_in, w_out, layer, conv_w, conv_b, dt_bias, a_log, d_skip, ssd_norm):
    n_main = 2 * SSD_INNER + 2 * SSD_GROUPS * SSD_STATE
    h = _norm(x, norm_g)
    w_in_t = jnp.swapaxes(w_in, 1, 2)
    z_act = _proj(h, w_in_t, layer, mode="silu", n_cols=SSD_INNER, out_dtype=F32, transposed=True)
    xbc = _proj(h, w_in_t, layer, mode="conv", n_cols=n_main - SSD_INNER, first_col=SSD_INNER,
                extra=(conv_w, conv_b.reshape(1, -1)), out_dtype=F32, transposed=True)
    dt_raw = _proj(h, w_in_t, layer, mode="plain", n_cols=LANES, first_col=n_main, out_dtype=F32, tn=LANES,
                   transposed=True)
    y = _ssd_scan(z_act, xbc, dt_raw, dt_bias, a_log, d_skip, ssd_norm)
    return _proj(y, w_out, layer, mode="residual", n_cols=x.shape[1], extra=(x,), out_dtype=F32, tn=512)


@jax.jit
def kernel(x, mem, rel_bias, ffn1_norm, ffn1_w_gate, ffn1_w_up, ffn1_w_down, mix_norm, attn_w_qkv, attn_w_o,
           attn_q_norm, attn_k_norm, attn_lambda, attn_subln, ssd_w_in, ssd_conv_w, ssd_conv_b, ssd_dt_bias,
           ssd_a_log, ssd_d, ssd_norm, ssd_w_out, xattn_norm, mem_norm, xattn_w_q, xattn_w_kv, xattn_w_o,
           xattn_q_norm, xattn_k_norm, ffn2_norm, ffn2_w_gate, ffn2_w_up, ffn2_w_down):
    b, s, d = x.shape
    assert b == 1
    t = min(ATTN_BLOCK, s)
    xs = x[0]
    mem2 = mem[0]
    bias = _bias_tiles(rel_bias, t)
    for i in range(DEPTH):
        xs = _ffn(xs, ffn1_norm[i], ffn1_w_gate, ffn1_w_up, ffn1_w_down, i)
        j = i // N_MIXERS
        if i % N_MIXERS == 0:
            lambda_init = 0.8 - 0.6 * math.exp(-0.3 * i)
            xs = _attn_layer(xs, mix_norm[i], attn_w_qkv, attn_w_o, j, attn_q_norm[j], attn_k_norm[j],
                             attn_lambda[j], attn_subln[j], rel_bias, bias, lambda_init, t)
        else:
            xs = _ssd_layer(xs, mix_norm[i], ssd_w_in, ssd_w_out, j, ssd_conv_w[j], ssd_conv_b[j], ssd_dt_bias[j],
                            ssd_a_log[j], ssd_d[j], ssd_norm[j])
        k, v = _mem_kv(mem2, mem_norm[i], xattn_w_kv, i, xattn_k_norm[i])
        xs = _xattn(xs, xattn_norm[i], xattn_w_q, xattn_q_norm[i], k, v, xattn_w_o, i)
        xs = _ffn(xs, ffn2_norm[i], ffn2_w_gate, ffn2_w_up, ffn2_w_down, i)
    return xs[None]
```

```python
import functools
import math

import jax
import jax.numpy as jnp
from jax import lax
from jax.experimental import pallas as pl
from jax.experimental.pallas import tpu as pltpu

F32 = jnp.float32
BF16 = jnp.bfloat16

EPS = 1e-6
NEG_INF = -1e30
LOG2E = math.log2(math.e)

DEPTH = 4
N_MIXERS = 2

DA_HEADS = 8
DA_HEAD_DIM = 128
DA_V_DIM = 2 * DA_HEAD_DIM
REL_BUCKETS = 32
REL_MAX_DIST = 128
ATTN_BLOCK = 512
FAR_GROUP = 4
MAX_UNSHIFTED_LOG2_LOGIT = 100.0

SSD_HEAD_DIM = 64
SSD_HEADS = 64
SSD_GROUPS = 8
SSD_STATE = 128
SSD_CONV = 4
SSD_CHUNK = 128
SSD_INNER = SSD_HEADS * SSD_HEAD_DIM
SSD_HEADS_PER_GROUP = SSD_HEADS // SSD_GROUPS
SSD_GROUP_WIDTH = SSD_HEADS_PER_GROUP * SSD_HEAD_DIM
SSD_COL_BLOCK = 2048
CONV_HALO = 8
FFN_NORM_SUB_ROWS = 256
PROJ_SUB_ROWS = 256

XA_HEADS = 4
XA_HEAD_DIM = 128

LANES = 128
VMEM_LIMIT_BYTES = 56 * 1024 * 1024


def _params(*sem):
    return pltpu.CompilerParams(dimension_semantics=sem, vmem_limit_bytes=VMEM_LIMIT_BYTES)


def _rms(x, g):
    return x * lax.rsqrt(jnp.mean(x * x, axis=-1, keepdims=True) + EPS) * g


def _silu(x):
    half = 0.5 * x
    return half + half * jnp.tanh(half)


def _dot(a, b):
    return jnp.dot(a, b, preferred_element_type=F32)


def _dot_nt(a, b):
    return lax.dot_general(a, b, (((1,), (1,)), ((), ())), preferred_element_type=F32)


def _dot_tn(a, b):
    return lax.dot_general(a, b, (((0,), (0,)), ((), ())), preferred_element_type=F32)


def _ffn_kernel(x_ref, g_ref, wg_ref, wu_ref, wd_ref, o_ref, h_ref):
    j = pl.program_id(1)

    def half_ffn(h, wg, wu, wd):
        gate = _dot(h, wg)
        up = _dot(h, wu)
        return _dot((0.5 * _silu(gate) * up).astype(BF16), wd)

    @pl.when(j == 0)
    def _():
        wg, wu, wd = wg_ref[...].astype(BF16), wu_ref[...].astype(BF16), wd_ref[...].astype(BF16)
        tm = x_ref.shape[0]
        sub = min(FFN_NORM_SUB_ROWS, tm)
        for r in range(tm // sub):
            rows = slice(r * sub, (r + 1) * sub)
            x = x_ref[rows, :]
            h = _rms(x, g_ref[...]).astype(BF16)
            h_ref[rows, :] = h
            o_ref[rows, :] = x + half_ffn(h, wg, wu, wd)

    @pl.when(j > 0)
    def _():
        o_ref[...] += half_ffn(h_ref[...], wg_ref[...].astype(BF16), wu_ref[...].astype(BF16),
                               wd_ref[...].astype(BF16))


def _ffn(x, g, wg, wu, wd, layer, *, tm=1024, tf=256):
    s, d = x.shape
    f = wg.shape[2]
    tm = min(tm, s)
    return pl.pallas_call(
        _ffn_kernel,
        grid=(s // tm, f // tf),
        in_specs=[
            pl.BlockSpec((tm, d), lambda i, j: (i, 0)),
            pl.BlockSpec((1, d), lambda i, j: (0, 0)),
            pl.BlockSpec((None, d, tf), lambda i, j: (layer, 0, j)),
            pl.BlockSpec((None, d, tf), lambda i, j: (layer, 0, j)),
            pl.BlockSpec((None, tf, d), lambda i, j: (layer, j, 0)),
        ],
        out_specs=pl.BlockSpec((tm, d), lambda i, j: (i, 0)),
        out_shape=jax.ShapeDtypeStruct((s, d), F32),
        scratch_shapes=[pltpu.VMEM((tm, d), BF16)],
        compiler_params=_params("parallel", "arbitrary"),
        name="ffn",
    )(x, g.reshape(1, d), wg, wu, wd)


def _norm_kernel(x_ref, g_ref, o_ref):
    o_ref[...] = _rms(x_ref[...], g_ref[...]).astype(o_ref.dtype)


def _norm(x, g, *, tm=1024):
    s, d = x.shape
    tm = min(tm, s)
    return pl.pallas_call(
        _norm_kernel,
        grid=(s // tm,),
        in_specs=[pl.BlockSpec((tm, d), lambda i: (i, 0)), pl.BlockSpec((1, d), lambda i: (0, 0))],
        out_specs=pl.BlockSpec((tm, d), lambda i: (i, 0)),
        out_shape=jax.ShapeDtypeStruct((s, d), BF16),
        compiler_params=_params("parallel"),
        name="pre_norm",
    )(x, g.reshape(1, d))


def _proj_kernel(*refs, mode, n_valid, transposed):
    if mode == "qk_norm":
        h_ref, w_ref, cg_ref, o_ref, wb_ref = refs
    elif mode == "residual":
        h_ref, w_ref, x_ref, o_ref, wb_ref = refs
    elif mode == "conv":
        h_ref, w_ref, cw_ref, cb_ref, o_ref, wb_ref, halo_ref = refs
    else:
        h_ref, w_ref, o_ref, wb_ref = refs
    j = pl.program_id(0)
    i = pl.program_id(1)

    @pl.when(i == 0)
    def _():
        w = w_ref[...]
        col_axis = 0 if transposed else 1
        if n_valid < w.shape[col_axis]:
            limit = jnp.where(j == pl.num_programs(0) - 1, n_valid, w.shape[col_axis])
            w = jnp.where(lax.broadcasted_iota(jnp.int32, w.shape, col_axis) < limit, w, 0.0)
        wb_ref[...] = w.astype(BF16)
        if mode == "conv":
            halo_ref[...] = jnp.zeros(halo_ref.shape, F32)

    tm = h_ref.shape[0]
    sub = min(PROJ_SUB_ROWS, tm)
    halo = halo_ref[...] if mode == "conv" else None
    for r in range(tm // sub):
        rows = slice(r * sub, (r + 1) * sub)
        y = (_dot_nt if transposed else _dot)(h_ref[rows, :], wb_ref[...])
        if mode == "qk_norm":
            for c in range(y.shape[1] // LANES):
                cols = slice(c * LANES, (c + 1) * LANES)
                o_ref[rows, cols] = _rms(y[:, cols], cg_ref[:, cols]).astype(o_ref.dtype)
        elif mode == "silu":
            o_ref[rows, :] = _silu(y).astype(o_ref.dtype)
        elif mode == "residual":
            o_ref[rows, :] = x_ref[rows, :] + y
        elif mode == "conv":
            ext = jnp.concatenate([halo, y], axis=0)
            acc = cb_ref[...] + cw_ref[SSD_CONV - 1:SSD_CONV, :] * y
            for k in range(SSD_CONV - 1):
                shifted = pltpu.roll(ext, SSD_CONV - 1 - k, axis=0)[CONV_HALO:, :]
                acc = acc + cw_ref[k:k + 1, :] * shifted
            halo = y[sub - CONV_HALO:, :]
            o_ref[rows, :] = _silu(acc).astype(o_ref.dtype)
        else:
            o_ref[rows, :] = y.astype(o_ref.dtype)
    if mode == "conv":
        halo_ref[...] = halo


def _proj(h, w, layer, *, mode, n_cols, first_col=0, extra=(), out_dtype, transposed=False, tm=1024, tn=1024):
    s, d = h.shape
    tm = min(tm, s)
    tn = min(tn, n_cols)
    assert n_cols % tn == 0 and first_col % tn == 0
    col0 = first_col // tn
    w_cols = w.shape[1] if transposed else w.shape[2]
    n_valid = min(tn, w_cols - first_col - (n_cols - tn))
    if transposed:
        w_spec = pl.BlockSpec((None, tn, d), lambda j, i: (layer, col0 + j, 0))
    else:
        w_spec = pl.BlockSpec((None, d, tn), lambda j, i: (layer, 0, col0 + j))
    in_specs = [pl.BlockSpec((tm, d), lambda j, i: (i, 0)), w_spec]
    if mode == "residual":
        in_specs.append(pl.BlockSpec((tm, tn), lambda j, i: (i, j)))
    else:
        in_specs += [pl.BlockSpec((e.shape[0], tn), lambda j, i: (0, j)) for e in extra]
    scratch = [pltpu.VMEM((tn, d) if transposed else (d, tn), BF16)]
    if mode == "conv":
        scratch.append(pltpu.VMEM((CONV_HALO, tn), F32))
    return pl.pallas_call(
        functools.partial(_proj_kernel, mode=mode, n_valid=n_valid, transposed=transposed),
        grid=(n_cols // tn, s // tm),
        in_specs=in_specs,
        out_specs=pl.BlockSpec((tm, tn), lambda j, i: (i, j)),
        out_shape=jax.ShapeDtypeStruct((s, n_cols), out_dtype),
        scratch_shapes=scratch,
        compiler_params=_params("arbitrary", "arbitrary"),
        name="proj_" + mode,
    )(h, w, *extra)


def _bias_kernel(tab_ref, o_ref, *, t):
    h = pl.program_id(0)
    row = lax.broadcasted_iota(jnp.int32, (t, t), 0)
    col = lax.broadcasted_iota(jnp.int32, (t, t), 1)
    max_exact = REL_BUCKETS // 2
    far = tab_ref[REL_BUCKETS - 1, h]
    for blk in range(2):
        dist = row - col + blk * t
        n = jnp.maximum(dist, 0)
        nf = jnp.maximum(n, 1).astype(F32)
        large = max_exact + (jnp.log(nf / max_exact) / math.log(REL_MAX_DIST / max_exact)
                             * (REL_BUCKETS - max_exact)).astype(jnp.int32)
        large = jnp.minimum(large, REL_BUCKETS - 1)
        bucket = jnp.where(n < max_exact, n, large)
        bias = jnp.zeros((t, t), F32)
        for b in range(REL_BUCKETS - 1):
            bias = jnp.where(bucket == b, (tab_ref[b, h] - far) * LOG2E, bias)
        if blk == 0:
            bias = jnp.where(dist >= 0, bias, NEG_INF)
        o_ref[0, :, (1 - blk) * t:(2 - blk) * t] = bias


def _bias_tiles(rel_bias, t):
    assert t >= REL_MAX_DIST
    return pl.pallas_call(
        functools.partial(_bias_kernel, t=t),
        grid=(DA_HEADS,),
        in_specs=[pl.BlockSpec(memory_space=pltpu.SMEM)],
        out_specs=pl.BlockSpec((1, t, 2 * t), lambda h: (h, 0, 0)),
        out_shape=jax.ShapeDtypeStruct((DA_HEADS, t, 2 * t), F32),
        compiler_params=_params("arbitrary"),
        name="rel_bias_tiles",
    )(rel_bias)


def _attn_kernel(q_ref, k_ref, v_ref, bias_ref, lv_ref, sg_ref, o_ref, m_ref, l_ref, acc_ref, *, t, lambda_init,
                 online):
    qb = pl.program_id(1)
    dh = DA_HEAD_DIM
    if online:
        m_ref[...] = jnp.full(m_ref.shape, NEG_INF, F32)
    l_ref[...] = jnp.zeros(l_ref.shape, F32)
    acc_ref[...] = jnp.zeros(acc_ref.shape, F32)

    def lane_chunk_sum(p):
        out = p[:, 0:LANES]
        for c in range(1, p.shape[1] // LANES):
            out = out + p[:, c * LANES:(c + 1) * LANES]
        return out

    def block(start, width, bias):
        v = v_ref[pl.ds(start, width), :]
        for mp in range(2):
            cols = slice(mp * dh, (mp + 1) * dh)
            s = _dot_nt(q_ref[:, cols], k_ref[pl.ds(start, width), cols])
            if bias is not None:
                s = s + bias
            if online:
                m_prev = m_ref[mp]
                m_new = jnp.maximum(m_prev, jnp.max(s, axis=-1, keepdims=True))
                alpha = jnp.exp2(m_prev - m_new)
                p = jnp.exp2(s - m_new)
                l_ref[mp] = alpha * l_ref[mp] + lane_chunk_sum(p)
                acc_ref[mp] = alpha * acc_ref[mp] + _dot(p.astype(BF16), v)
                m_ref[mp] = m_new
            else:
                p = jnp.exp2(s)
                l_ref[mp] += lane_chunk_sum(p)
                acc_ref[mp] += _dot(p.astype(BF16), v)

    n_far = jnp.maximum(qb - 1, 0)
    n_groups = n_far // FAR_GROUP

    def far_group(i, carry):
        block(pl.multiple_of(i * (FAR_GROUP * t), FAR_GROUP * t), FAR_GROUP * t, None)
        return carry

    def far_single(i, carry):
        block(pl.multiple_of((n_groups * FAR_GROUP + i) * t, t), t, None)
        return carry

    lax.fori_loop(0, n_groups, far_group, 0)
    lax.fori_loop(0, n_far - n_groups * FAR_GROUP, far_single, 0)

    @pl.when(qb > 0)
    def _():
        block(pl.multiple_of((qb - 1) * t, t), 2 * t, bias_ref[0])

    @pl.when(qb == 0)
    def _():
        block(0, t, bias_ref[0, :, t:2 * t])

    lv = lv_ref[...]
    lam = (jnp.exp(jnp.sum(lv[0:1] * lv[1:2], axis=-1, keepdims=True))
           - jnp.exp(jnp.sum(lv[2:3] * lv[3:4], axis=-1, keepdims=True)) + lambda_init)
    l1 = jnp.sum(l_ref[0], axis=-1, keepdims=True)
    l2 = jnp.sum(l_ref[1], axis=-1, keepdims=True)
    o = acc_ref[0] / l1 - lam * (acc_ref[1] / l2)
    o_ref[...] = (_rms(o, sg_ref[...]) * (1.0 - lambda_init)).astype(o_ref.dtype)


def _diff_attention(qk, v, bias, lam_vecs, subln_gain, lambda_init, logit_bound, *, t):
    s = qk.shape[0]
    t = min(t, s)
    assert s >= max(2, FAR_GROUP) * t
    hw = DA_V_DIM

    def call(online):
        return pl.pallas_call(
            functools.partial(_attn_kernel, t=t, lambda_init=lambda_init, online=online),
            grid=(DA_HEADS, s // t),
            in_specs=[
                pl.BlockSpec((t, hw), lambda h, i: (i, h)),
                pl.BlockSpec((s, hw), lambda h, i: (0, DA_HEADS + h)),
                pl.BlockSpec((s, hw), lambda h, i: (0, h)),
                pl.BlockSpec((1, t, 2 * t), lambda h, i: (h, 0, 0)),
                pl.BlockSpec((4, DA_HEAD_DIM), lambda h, i: (0, 0)),
                pl.BlockSpec((1, hw), lambda h, i: (0, 0)),
            ],
            out_specs=pl.BlockSpec((t, hw), lambda h, i: (i, h)),
            out_shape=jax.ShapeDtypeStruct((s, DA_HEADS * hw), BF16),
            scratch_shapes=[
                pltpu.VMEM((2, t, 1), F32),
                pltpu.VMEM((2, t, LANES), F32),
                pltpu.VMEM((2, t, hw), F32),
            ],
            compiler_params=_params("parallel", "arbitrary"),
            name="diff_attn_online" if online else "diff_attn",
        )

    operands = (qk, qk, v, bias, lam_vecs, subln_gain.reshape(1, hw))
    return lax.cond(logit_bound < MAX_UNSHIFTED_LOG2_LOGIT,
                    lambda *a: call(False)(*a), lambda *a: call(True)(*a), *operands)


def _ssd_kernel(z0_ref, z1_ref, x0_ref, x1_ref, bc_ref, dt_ref, dtb_ref, alog_ref, dskip_ref,
                ng_ref, o_ref, bcs_ref, y_ref, state_ref):
    c = pl.program_id(0)
    ln = SSD_CHUNK
    cbw = SSD_COL_BLOCK
    ns = SSD_STATE

    @pl.when(c == 0)
    def _():
        state_ref[...] = jnp.zeros(state_ref.shape, F32)

    bcs_ref[...] = bc_ref[...].astype(BF16)
    x_refs = (x0_ref, x1_ref)

    dt = jax.nn.softplus(dt_ref[:, 0:SSD_HEADS] + dtb_ref[...])
    a = dt * (-LOG2E * jnp.exp(alog_ref[...]))
    row = lax.broadcasted_iota(jnp.int32, (ln, ln), 0)
    col = lax.broadcasted_iota(jnp.int32, (ln, ln), 1)
    causal = row >= col
    acum = jnp.dot(causal.astype(F32), a, preferred_element_type=F32, precision=lax.Precision.HIGHEST)
    acum_t = acum.T
    lane_lo = lax.broadcasted_iota(jnp.int32, (ln, LANES), 1) < SSD_HEAD_DIM

    for g in range(SSD_GROUPS):
        b_g = bcs_ref[:, g * ns:(g + 1) * ns]
        c_g = bcs_ref[:, SSD_GROUPS * ns + g * ns:SSD_GROUPS * ns + (g + 1) * ns]
        cb = _dot_nt(c_g, b_g)
        y_state = _dot(c_g, state_ref[g].astype(BF16))
        xw_parts = []
        decay_parts = []
        for pr in range(SSD_HEADS_PER_GROUP // 2):
            h0 = g * SSD_HEADS_PER_GROUP + 2 * pr
            ch = slice(h0 * SSD_HEAD_DIM, h0 * SSD_HEAD_DIM + LANES)
            m_parts = []
            a_cols = []
            for h in (h0, h0 + 1):
                a_col = jnp.broadcast_to(acum[:, h:h + 1], (ln, LANES))
                seg = a_col - acum_t[h:h + 1, :]
                m_parts.append((cb * jnp.where(causal, jnp.exp2(seg), 0.0)).astype(BF16))
                a_cols.append(a_col)
            a_pair = jnp.where(lane_lo, a_cols[0], a_cols[1])
            dt_pair = jnp.where(lane_lo, dt[:, h0:h0 + 1], dt[:, h0 + 1:h0 + 2])
            x_pair = x_refs[ch.start // cbw][:, ch.start % cbw:ch.start % cbw + LANES]
            xdt = x_pair * dt_pair
            xdt_b = xdt.astype(BF16)
            zero = jnp.zeros_like(xdt_b)
            rhs = jnp.concatenate([jnp.where(lane_lo, xdt_b, zero), jnp.where(lane_lo, zero, xdt_b)], axis=0)
            y = _dot(jnp.concatenate(m_parts, axis=1), rhs)
            y = y + y_state[:, pr * LANES:(pr + 1) * LANES] * jnp.exp2(a_pair)
            y_ref[:, ch] = y + x_pair * dskip_ref[:, ch]
            a_last = a_pair[ln - 1:ln, :]
            xw_parts.append((xdt * jnp.exp2(a_last - a_pair)).astype(BF16))
            decay_parts.append(jnp.exp2(a_last))
        xw = jnp.concatenate(xw_parts, axis=1)
        decay = jnp.concatenate(decay_parts, axis=1)
        state_ref[g] = state_ref[g] * decay + _dot_tn(b_g, xw)

    gated = jnp.concatenate([y_ref[:, 0:cbw] * z0_ref[...], y_ref[:, cbw:2 * cbw] * z1_ref[...]], axis=1)
    o_ref[...] = _rms(gated, ng_ref[...]).astype(o_ref.dtype)


def _ssd_scan(z_act, xbc, dt_raw, dt_bias, a_log, d_skip, norm_g):
    s = z_act.shape[0]
    ln = SSD_CHUNK
    cbw = SSD_COL_BLOCK
    assert SSD_INNER == 2 * cbw and 2 * SSD_GROUPS * SSD_STATE == cbw

    def col_block(j):
        return pl.BlockSpec((ln, cbw), lambda c: (c, j))

    def whole(shape):
        return pl.BlockSpec(shape, lambda c: (0,) * len(shape))

    return pl.pallas_call(
        _ssd_kernel,
        grid=(s // ln,),
        in_specs=[
            col_block(0), col_block(1), col_block(0), col_block(1), col_block(2),
            pl.BlockSpec((ln, LANES), lambda c: (c, 0)),
            whole((1, SSD_HEADS)), whole((1, SSD_HEADS)),
            whole((1, SSD_INNER)), whole((1, SSD_INNER)),
        ],
        out_specs=pl.BlockSpec((ln, SSD_INNER), lambda c: (c, 0)),
        out_shape=jax.ShapeDtypeStruct((s, SSD_INNER), BF16),
        scratch_shapes=[
            pltpu.VMEM((ln, 2 * SSD_GROUPS * SSD_STATE), BF16),
            pltpu.VMEM((ln, SSD_INNER), F32),
            pltpu.VMEM((SSD_GROUPS, SSD_STATE, SSD_GROUP_WIDTH), F32),
        ],
        compiler_params=_params("arbitrary"),
        name="ssd_scan",
    )(z_act, z_act, xbc, xbc, xbc, dt_raw,
      dt_bias.reshape(1, SSD_HEADS), a_log.reshape(1, SSD_HEADS),
      jnp.repeat(d_skip, SSD_HEAD_DIM).reshape(1, SSD_INNER), norm_g.reshape(1, SSD_INNER))


def _head_rms(y, gain, n_heads):
    return [_rms(y[:, h * XA_HEAD_DIM:(h + 1) * XA_HEAD_DIM], gain) for h in range(n_heads)]


def _mem_kv_kernel(mem_ref, g_ref, w_ref, kg_ref, k_ref, v_ref):
    xw = XA_HEADS * XA_HEAD_DIM
    kv = _dot(_rms(mem_ref[...], g_ref[...]).astype(BF16), w_ref[...].astype(BF16))
    k_ref[...] = jnp.concatenate(_head_rms(kv[:, 0:xw], kg_ref[...], XA_HEADS), axis=1).astype(BF16)
    v_ref[...] = kv[:, xw:2 * xw].astype(BF16)


def _mem_kv(mem, g, w_kv, layer, k_gain):
    m, d = mem.shape
    xw = XA_HEADS * XA_HEAD_DIM

    def whole(shape):
        return pl.BlockSpec(shape, lambda i: (0,) * len(shape))

    return pl.pallas_call(
        _mem_kv_kernel,
        grid=(1,),
        in_specs=[whole((m, d)), whole((1, d)), pl.BlockSpec((None, d, 2 * xw), lambda i: (layer, 0, 0)),
                  whole((1, XA_HEAD_DIM))],
        out_specs=(whole((m, xw)), whole((m, xw))),
        out_shape=(jax.ShapeDtypeStruct((m, xw), BF16), jax.ShapeDtypeStruct((m, xw), BF16)),
        compiler_params=_params("arbitrary"),
        name="mem_kv",
    )(mem, g.reshape(1, d), w_kv, k_gain.reshape(1, XA_HEAD_DIM))


def _xattn_kernel(x_ref, g_ref, wq_ref, qg_ref, k_ref, v_ref, wo_ref, o_ref, wq_b_ref, wo_b_ref):
    @pl.when(pl.program_id(0) == 0)
    def _():
        wq_b_ref[...] = wq_ref[...].astype(BF16)
        wo_b_ref[...] = wo_ref[...].astype(BF16)

    x = x_ref[...]
    q = _dot(_rms(x, g_ref[...]).astype(BF16), wq_b_ref[...])
    heads = []
    for h, qh in enumerate(_head_rms(q, qg_ref[...], XA_HEADS)):
        cols = slice(h * XA_HEAD_DIM, (h + 1) * XA_HEAD_DIM)
        s = _dot_nt((qh * XA_HEAD_DIM ** -0.5).astype(BF16), k_ref[:, cols])
        p = jnp.exp(s - jnp.max(s, axis=-1, keepdims=True))
        p = p / jnp.sum(p, axis=-1, keepdims=True)
        heads.append(_dot(p.astype(BF16), v_ref[:, cols]))
    o = jnp.concatenate(heads, axis=1).astype(BF16)
    o_ref[...] = x + _dot(o, wo_b_ref[...])


def _xattn(x, g, w_q, q_gain, k, v, w_o, layer, *, tm=512):
    s, d = x.shape
    m, xw = k.shape
    tm = min(tm, s)

    def whole(shape):
        return pl.BlockSpec(shape, lambda i: (0,) * len(shape))

    return pl.pallas_call(
        _xattn_kernel,
        grid=(s // tm,),
        in_specs=[
            pl.BlockSpec((tm, d), lambda i: (i, 0)),
            whole((1, d)), pl.BlockSpec((None, d, xw), lambda i: (layer, 0, 0)), whole((1, XA_HEAD_DIM)),
            whole((m, xw)), whole((m, xw)), pl.BlockSpec((None, xw, d), lambda i: (layer, 0, 0)),
        ],
        out_specs=pl.BlockSpec((tm, d), lambda i: (i, 0)),
        out_shape=jax.ShapeDtypeStruct((s, d), F32),
        scratch_shapes=[pltpu.VMEM((d, xw), BF16), pltpu.VMEM((xw, d), BF16)],
        compiler_params=_params("arbitrary"),
        name="mem_xattn",
    )(x, g.reshape(1, d), w_q, q_gain.reshape(1, XA_HEAD_DIM), k, v, w_o)


def _attn_layer(x, norm_g, w_qkv, w_o, layer, q_gain, k_gain, lam_vecs, subln_gain, rel_bias, bias, lambda_init, t):
    d = x.shape[1]
    n_maps = 2 * DA_HEADS
    q_col_gain = q_gain * (DA_HEAD_DIM ** -0.5 * LOG2E)
    col_gain = jnp.concatenate([jnp.tile(q_col_gain, n_maps), jnp.tile(k_gain, n_maps)]).reshape(1, 2 * d)
    h = _norm(x, norm_g)
    qk = _proj(h, w_qkv, layer, mode="qk_norm", n_cols=2 * d, extra=(col_gain,), out_dtype=BF16)
    v = _proj(h, w_qkv, layer, mode="plain", n_cols=d, first_col=2 * d, out_dtype=BF16)
    logit_bound = (DA_HEAD_DIM * jnp.max(jnp.abs(q_col_gain)) * jnp.max(jnp.abs(k_gain))
                   + LOG2E * jnp.max(jnp.abs(rel_bias - rel_bias[REL_BUCKETS - 1])))
    o = _diff_attention(qk, v, bias, lam_vecs, subln_gain, lambda_init, logit_bound, t=t)
    return _proj(o, w_o, layer, mode="residual", n_cols=d, extra=(x,), out_dtype=F32)


def _ssd_layer(x, norm_g, w_in, w_out, layer, conv_w, conv_b, dt_bias, a_log, d_skip, ssd_norm):
    n_main = 2 * SSD_INNER + 2 * SSD_GROUPS * SSD_STATE
    h = _norm(x, norm_g)
    w_in_t = jnp.swapaxes(w_in, 1, 2)
    z_act = _proj(h, w_in_t, layer, mode="silu", n_cols=SSD_INNER, out_dtype=F32, transposed=True)
    xbc = _proj(h, w_in_t, layer, mode="conv", n_cols=n_main - SSD_INNER, first_col=SSD_INNER,
                extra=(conv_w, conv_b.reshape(1, -1)), out_dtype=F32, transposed=True)
    dt_raw = _proj(h, w_in_t, layer, mode="plain", n_cols=LANES, first_col=n_main, out_dtype=F32, tn=LANES,
                   transposed=True)
    y = _ssd_scan(z_act, xbc, dt_raw, dt_bias, a_log, d_skip, ssd_norm)
    return _proj(y, w_out, layer, mode="residual", n_cols=x.shape[1], extra=(x,), out_dtype=F32, tn=512)


@jax.jit
def kernel(x, mem, rel_bias, ffn1_norm, ffn1_w_gate, ffn1_w_up, ffn1_w_down, mix_norm, attn_w_qkv, attn_w_o,
           attn_q_norm, attn_k_norm, attn_lambda, attn_subln, ssd_w_in, ssd_conv_w, ssd_conv_b, ssd_dt_bias,
           ssd_a_log, ssd_d, ssd_norm, ssd_w_out, xattn_norm, mem_norm, xattn_w_q, xattn_w_kv, xattn_w_o,
           xattn_q_norm, xattn_k_norm, ffn2_norm, ffn2_w_gate, ffn2_w_up, ffn2_w_down):
    b, s, d = x.shape
    assert b == 1
    t = min(ATTN_BLOCK, s)
    xs = x[0]
    mem2 = mem[0]
    bias = _bias_tiles(rel_bias, t)
    for i in range(DEPTH):
        xs = _ffn(xs, ffn1_norm[i], ffn1_w_gate, ffn1_w_up, ffn1_w_down, i)
        j = i // N_MIXERS
        if i % N_MIXERS == 0:
            lambda_init = 0.8 - 0.6 * math.exp(-0.3 * i)
            xs = _attn_layer(xs, mix_norm[i], attn_w_qkv, attn_w_o, j, attn_q_norm[j], attn_k_norm[j],
                             attn_lambda[j], attn_subln[j], rel_bias, bias, lambda_init, t)
        else:
            xs = _ssd_layer(xs, mix_norm[i], ssd_w_in, ssd_w_out, j, ssd_conv_w[j], ssd_conv_b[j], ssd_dt_bias[j],
                            ssd_a_log[j], ssd_d[j], ssd_norm[j])
        k, v = _mem_kv(mem2, mem_norm[i], xattn_w_kv, i, xattn_k_norm[i])
        xs = _xattn(xs, xattn_norm[i], xattn_w_q, xattn_q_norm[i], k, v, xattn_w_o, i)
        xs = _ffn(xs, ffn2_norm[i], ffn2_w_gate, ffn2_w_up, ffn2_w_down, i)
    return xs[None]
```

```python
import functools
import math

import jax
import jax.numpy as jnp
from jax import lax
from jax.experimental import pallas as pl
from jax.experimental.pallas import tpu as pltpu

F32 = jnp.float32
BF16 = jnp.bfloat16

EPS = 1e-6
NEG_INF = -1e30
LOG2E = math.log2(math.e)

DEPTH = 4
N_MIXERS = 2

DA_HEADS = 8
DA_HEAD_DIM = 128
DA_V_DIM = 2 * DA_HEAD_DIM
REL_BUCKETS = 32
REL_MAX_DIST = 128
ATTN_BLOCK = 512
FAR_GROUP = 4
MAX_UNSHIFTED_LOG2_LOGIT = 100.0

SSD_HEAD_DIM = 64
SSD_HEADS = 64
SSD_GROUPS = 8
SSD_STATE = 128
SSD_CONV = 4
SSD_CHUNK = 128
SSD_INNER = SSD_HEADS * SSD_HEAD_DIM
SSD_HEADS_PER_GROUP = SSD_HEADS // SSD_GROUPS
SSD_GROUP_WIDTH = SSD_HEADS_PER_GROUP * SSD_HEAD_DIM
SSD_COL_BLOCK = 2048
CONV_HALO = 8
PROJ_SUB_ROWS = 256

XA_HEADS = 4
XA_HEAD_DIM = 128

LANES = 128
VMEM_LIMIT_BYTES = 56 * 1024 * 1024


def _params(*sem):
    return pltpu.CompilerParams(dimension_semantics=sem, vmem_limit_bytes=VMEM_LIMIT_BYTES)


def _rms(x, g):
    return x * lax.rsqrt(jnp.mean(x * x, axis=-1, keepdims=True) + EPS) * g


def _silu(x):
    half = 0.5 * x
    return half + half * jnp.tanh(half)


def _dot(a, b):
    return jnp.dot(a, b, preferred_element_type=F32)


def _dot_nt(a, b):
    return lax.dot_general(a, b, (((1,), (1,)), ((), ())), preferred_element_type=F32)


def _dot_tn(a, b):
    return lax.dot_general(a, b, (((0,), (0,)), ((), ())), preferred_element_type=F32)


def _ffn_kernel(x_ref, g_ref, wg_ref, wu_ref, wd_ref, o_ref, h_ref):
    j = pl.program_id(1)

    @pl.when(j == 0)
    def _():
        h_ref[...] = _rms(x_ref[...], g_ref[...]).astype(BF16)
        o_ref[...] = jnp.zeros_like(o_ref)

    h = h_ref[...]
    gate = _dot(h, wg_ref[...].astype(BF16))
    up = _dot(h, wu_ref[...].astype(BF16))
    a = (_silu(gate) * up).astype(BF16)
    o_ref[...] += _dot(a, wd_ref[...].astype(BF16))

    @pl.when(j == pl.num_programs(1) - 1)
    def _():
        o_ref[...] = x_ref[...] + 0.5 * o_ref[...]


def _ffn(x, g, wg, wu, wd, layer, *, tm=1024, tf=256):
    s, d = x.shape
    f = wg.shape[2]
    tm = min(tm, s)
    return pl.pallas_call(
        _ffn_kernel,
        grid=(s // tm, f // tf),
        in_specs=[
            pl.BlockSpec((tm, d), lambda i, j: (i, 0)),
            pl.BlockSpec((1, d), lambda i, j: (0, 0)),
            pl.BlockSpec((None, d, tf), lambda i, j: (layer, 0, j)),
            pl.BlockSpec((None, d, tf), lambda i, j: (layer, 0, j)),
            pl.BlockSpec((None, tf, d), lambda i, j: (layer, j, 0)),
        ],
        out_specs=pl.BlockSpec((tm, d), lambda i, j: (i, 0)),
        out_shape=jax.ShapeDtypeStruct((s, d), F32),
        scratch_shapes=[pltpu.VMEM((tm, d), BF16)],
        compiler_params=_params("parallel", "arbitrary"),
        name="ffn",
    )(x, g.reshape(1, d), wg, wu, wd)


def _norm_kernel(x_ref, g_ref, o_ref):
    o_ref[...] = _rms(x_ref[...], g_ref[...]).astype(o_ref.dtype)


def _norm(x, g, *, tm=1024):
    s, d = x.shape
    tm = min(tm, s)
    return pl.pallas_call(
        _norm_kernel,
        grid=(s // tm,),
        in_specs=[pl.BlockSpec((tm, d), lambda i: (i, 0)), pl.BlockSpec((1, d), lambda i: (0, 0))],
        out_specs=pl.BlockSpec((tm, d), lambda i: (i, 0)),
        out_shape=jax.ShapeDtypeStruct((s, d), BF16),
        compiler_params=_params("parallel"),
        name="pre_norm",
    )(x, g.reshape(1, d))


def _proj_kernel(*refs, mode, n_valid, transposed):
    if mode == "qk_norm":
        h_ref, w_ref, cg_ref, o_ref, wb_ref = refs
    elif mode == "residual":
        h_ref, w_ref, x_ref, o_ref, wb_ref = refs
    elif mode == "conv":
        h_ref, w_ref, cw_ref, cb_ref, o_ref, wb_ref, halo_ref = refs
    else:
        h_ref, w_ref, o_ref, wb_ref = refs
    j = pl.program_id(0)
    i = pl.program_id(1)

    @pl.when(i == 0)
    def _():
        w = w_ref[...]
        col_axis = 0 if transposed else 1
        if n_valid < w.shape[col_axis]:
            limit = jnp.where(j == pl.num_programs(0) - 1, n_valid, w.shape[col_axis])
            w = jnp.where(lax.broadcasted_iota(jnp.int32, w.shape, col_axis) < limit, w, 0.0)
        wb_ref[...] = w.astype(BF16)
        if mode == "conv":
            halo_ref[...] = jnp.zeros(halo_ref.shape, F32)

    tm = h_ref.shape[0]
    sub = min(PROJ_SUB_ROWS, tm)
    halo = halo_ref[...] if mode == "conv" else None
    for r in range(tm // sub):
        rows = slice(r * sub, (r + 1) * sub)
        y = (_dot_nt if transposed else _dot)(h_ref[rows, :], wb_ref[...])
        if mode == "qk_norm":
            for c in range(y.shape[1] // LANES):
                cols = slice(c * LANES, (c + 1) * LANES)
                o_ref[rows, cols] = _rms(y[:, cols], cg_ref[:, cols]).astype(o_ref.dtype)
        elif mode == "silu":
            o_ref[rows, :] = _silu(y).astype(o_ref.dtype)
        elif mode == "residual":
            o_ref[rows, :] = x_ref[rows, :] + y
        elif mode == "conv":
            ext = jnp.concatenate([halo, y], axis=0)
            acc = cb_ref[...] + cw_ref[SSD_CONV - 1:SSD_CONV, :] * y
            for k in range(SSD_CONV - 1):
                shifted = pltpu.roll(ext, SSD_CONV - 1 - k, axis=0)[CONV_HALO:, :]
                acc = acc + cw_ref[k:k + 1, :] * shifted
            halo = y[sub - CONV_HALO:, :]
            o_ref[rows, :] = _silu(acc).astype(o_ref.dtype)
        else:
            o_ref[rows, :] = y.astype(o_ref.dtype)
    if mode == "conv":
        halo_ref[...] = halo


def _proj(h, w, layer, *, mode, n_cols, first_col=0, extra=(), out_dtype, transposed=False, tm=1024, tn=1024):
    s, d = h.shape
    tm = min(tm, s)
    tn = min(tn, n_cols)
    assert n_cols % tn == 0 and first_col % tn == 0
    col0 = first_col // tn
    w_cols = w.shape[1] if transposed else w.shape[2]
    n_valid = min(tn, w_cols - first_col - (n_cols - tn))
    if transposed:
        w_spec = pl.BlockSpec((None, tn, d), lambda j, i: (layer, col0 + j, 0))
    else:
        w_spec = pl.BlockSpec((None, d, tn), lambda j, i: (layer, 0, col0 + j))
    in_specs = [pl.BlockSpec((tm, d), lambda j, i: (i, 0)), w_spec]
    if mode == "residual":
        in_specs.append(pl.BlockSpec((tm, tn), lambda j, i: (i, j)))
    else:
        in_specs += [pl.BlockSpec((e.shape[0], tn), lambda j, i: (0, j)) for e in extra]
    scratch = [pltpu.VMEM((tn, d) if transposed else (d, tn), BF16)]
    if mode == "conv":
        scratch.append(pltpu.VMEM((CONV_HALO, tn), F32))
    return pl.pallas_call(
        functools.partial(_proj_kernel, mode=mode, n_valid=n_valid, transposed=transposed),
        grid=(n_cols // tn, s // tm),
        in_specs=in_specs,
        out_specs=pl.BlockSpec((tm, tn), lambda j, i: (i, j)),
        out_shape=jax.ShapeDtypeStruct((s, n_cols), out_dtype),
        scratch_shapes=scratch,
        compiler_params=_params("arbitrary", "arbitrary"),
        name="proj_" + mode,
    )(h, w, *extra)


def _bias_kernel(tab_ref, o_ref, *, t):
    h = pl.program_id(0)
    row = lax.broadcasted_iota(jnp.int32, (t, t), 0)
    col = lax.broadcasted_iota(jnp.int32, (t, t), 1)
    max_exact = REL_BUCKETS // 2
    far = tab_ref[REL_BUCKETS - 1, h]
    for blk in range(2):
        dist = row - col + blk * t
        n = jnp.maximum(dist, 0)
        nf = jnp.maximum(n, 1).astype(F32)
        large = max_exact + (jnp.log(nf / max_exact) / math.log(REL_MAX_DIST / max_exact)
                             * (REL_BUCKETS - max_exact)).astype(jnp.int32)
        large = jnp.minimum(large, REL_BUCKETS - 1)
        bucket = jnp.where(n < max_exact, n, large)
        bias = jnp.zeros((t, t), F32)
        for b in range(REL_BUCKETS - 1):
            bias = jnp.where(bucket == b, (tab_ref[b, h] - far) * LOG2E, bias)
        if blk == 0:
            bias = jnp.where(dist >= 0, bias, NEG_INF)
        o_ref[0, :, (1 - blk) * t:(2 - blk) * t] = bias


def _bias_tiles(rel_bias, t):
    assert t >= REL_MAX_DIST
    return pl.pallas_call(
        functools.partial(_bias_kernel, t=t),
        grid=(DA_HEADS,),
        in_specs=[pl.BlockSpec(memory_space=pltpu.SMEM)],
        out_specs=pl.BlockSpec((1, t, 2 * t), lambda h: (h, 0, 0)),
        out_shape=jax.ShapeDtypeStruct((DA_HEADS, t, 2 * t), F32),
        compiler_params=_params("arbitrary"),
        name="rel_bias_tiles",
    )(rel_bias)


def _attn_kernel(q_ref, k_ref, v_ref, bias_ref, lv_ref, sg_ref, o_ref, m_ref, l_ref, acc_ref, *, t, lambda_init,
                 online):
    qb = pl.program_id(1)
    dh = DA_HEAD_DIM
    if online:
        m_ref[...] = jnp.full(m_ref.shape, NEG_INF, F32)
    l_ref[...] = jnp.zeros(l_ref.shape, F32)
    acc_ref[...] = jnp.zeros(acc_ref.shape, F32)

    def lane_chunk_sum(p):
        out = p[:, 0:LANES]
        for c in range(1, p.shape[1] // LANES):
            out = out + p[:, c * LANES:(c + 1) * LANES]
        return out

    def block(start, width, bias):
        v = v_ref[pl.ds(start, width), :]
        for mp in range(2):
            cols = slice(mp * dh, (mp + 1) * dh)
            s = _dot_nt(q_ref[:, cols], k_ref[pl.ds(start, width), cols])
            if bias is not None:
                plain = width - bias.shape[1]
                biased = s[:, plain:] + bias
                s = biased if plain == 0 else jnp.concatenate([s[:, :plain], biased], axis=1)
            if online:
                m_prev = m_ref[mp]
                m_new = jnp.maximum(m_prev, jnp.max(s, axis=-1, keepdims=True))
                alpha = jnp.exp2(m_prev - m_new)
                p = jnp.exp2(s - m_new)
                l_ref[mp] = alpha * l_ref[mp] + lane_chunk_sum(p)
                acc_ref[mp] = alpha * acc_ref[mp] + _dot(p.astype(BF16), v)
                m_ref[mp] = m_new
            else:
                p = jnp.exp2(s)
                l_ref[mp] += lane_chunk_sum(p)
                acc_ref[mp] += _dot(p.astype(BF16), v)

    n_far = jnp.maximum(qb - 1, 0)
    n_groups = n_far // FAR_GROUP
    n_left = n_far - n_groups * FAR_GROUP

    def far_group(i, carry):
        block(pl.multiple_of(i * (FAR_GROUP * t), FAR_GROUP * t), FAR_GROUP * t, None)
        return carry

    lax.fori_loop(0, n_groups, far_group, 0)

    for left in range(FAR_GROUP):
        @pl.when(jnp.logical_and(qb > 0, n_left == left))
        def _(left=left):
            block(pl.multiple_of((qb - 1 - left) * t, t), (left + 2) * t, bias_ref[0])

    @pl.when(qb == 0)
    def _():
        block(0, t, bias_ref[0, :, t:2 * t])

    lv = lv_ref[...]
    lam = (jnp.exp(jnp.sum(lv[0:1] * lv[1:2], axis=-1, keepdims=True))
           - jnp.exp(jnp.sum(lv[2:3] * lv[3:4], axis=-1, keepdims=True)) + lambda_init)
    l1 = jnp.sum(l_ref[0], axis=-1, keepdims=True)
    l2 = jnp.sum(l_ref[1], axis=-1, keepdims=True)
    o = acc_ref[0] / l1 - lam * (acc_ref[1] / l2)
    o_ref[...] = (_rms(o, sg_ref[...]) * (1.0 - lambda_init)).astype(o_ref.dtype)


def _diff_attention(qk, v, bias, lam_vecs, subln_gain, lambda_init, logit_bound, *, t):
    s = qk.shape[0]
    t = min(t, s)
    assert s >= (FAR_GROUP + 1) * t
    hw = DA_V_DIM

    def call(online):
        return pl.pallas_call(
            functools.partial(_attn_kernel, t=t, lambda_init=lambda_init, online=online),
            grid=(DA_HEADS, s // t),
            in_specs=[
                pl.BlockSpec((t, hw), lambda h, i: (i, h)),
                pl.BlockSpec((s, hw), lambda h, i: (0, DA_HEADS + h)),
                pl.BlockSpec((s, hw), lambda h, i: (0, h)),
                pl.BlockSpec((1, t, 2 * t), lambda h, i: (h, 0, 0)),
                pl.BlockSpec((4, DA_HEAD_DIM), lambda h, i: (0, 0)),
                pl.BlockSpec((1, hw), lambda h, i: (0, 0)),
            ],
            out_specs=pl.BlockSpec((t, hw), lambda h, i: (i, h)),
            out_shape=jax.ShapeDtypeStruct((s, DA_HEADS * hw), BF16),
            scratch_shapes=[
                pltpu.VMEM((2, t, 1), F32),
                pltpu.VMEM((2, t, LANES), F32),
                pltpu.VMEM((2, t, hw), F32),
            ],
            compiler_params=_params("parallel", "arbitrary"),
            name="diff_attn_online" if online else "diff_attn",
        )

    operands = (qk, qk, v, bias, lam_vecs, subln_gain.reshape(1, hw))
    return lax.cond(logit_bound < MAX_UNSHIFTED_LOG2_LOGIT,
                    lambda *a: call(False)(*a), lambda *a: call(True)(*a), *operands)


def _ssd_kernel(z0_ref, z1_ref, x0_ref, x1_ref, bc_ref, dt_ref, dtb_ref, alog_ref, dskip_ref,
                ng_ref, o_ref, bcs_ref, y_ref, state_ref):
    c = pl.program_id(0)
    ln = SSD_CHUNK
    cbw = SSD_COL_BLOCK
    ns = SSD_STATE

    @pl.when(c == 0)
    def _():
        state_ref[...] = jnp.zeros(state_ref.shape, F32)

    bcs_ref[...] = bc_ref[...].astype(BF16)
    x_refs = (x0_ref, x1_ref)

    dt = jax.nn.softplus(dt_ref[:, 0:SSD_HEADS] + dtb_ref[...])
    a = dt * (-LOG2E * jnp.exp(alog_ref[...]))
    row = lax.broadcasted_iota(jnp.int32, (ln, ln), 0)
    col = lax.broadcasted_iota(jnp.int32, (ln, ln), 1)
    causal = row >= col
    acum = jnp.dot(causal.astype(F32), a, preferred_element_type=F32, precision=lax.Precision.HIGHEST)
    acum_t = acum.T
    lane_lo = lax.broadcasted_iota(jnp.int32, (ln, LANES), 1) < SSD_HEAD_DIM

    for g in range(SSD_GROUPS):
        b_g = bcs_ref[:, g * ns:(g + 1) * ns]
        c_g = bcs_ref[:, SSD_GROUPS * ns + g * ns:SSD_GROUPS * ns + (g + 1) * ns]
        cb = _dot_nt(c_g, b_g)
        y_state = _dot(c_g, state_ref[g].astype(BF16))
        xw_parts = []
        decay_parts = []
        for pr in range(SSD_HEADS_PER_GROUP // 2):
            h0 = g * SSD_HEADS_PER_GROUP + 2 * pr
            ch = slice(h0 * SSD_HEAD_DIM, h0 * SSD_HEAD_DIM + LANES)
            m_parts = []
            a_cols = []
            for h in (h0, h0 + 1):
                a_col = jnp.broadcast_to(acum[:, h:h + 1], (ln, LANES))
                seg = a_col - acum_t[h:h + 1, :]
                m_parts.append((cb * jnp.where(causal, jnp.exp2(seg), 0.0)).astype(BF16))
                a_cols.append(a_col)
            a_pair = jnp.where(lane_lo, a_cols[0], a_cols[1])
            dt_pair = jnp.where(lane_lo, dt[:, h0:h0 + 1], dt[:, h0 + 1:h0 + 2])
            x_pair = x_refs[ch.start // cbw][:, ch.start % cbw:ch.start % cbw + LANES]
            xdt = x_pair * dt_pair
            xdt_b = xdt.astype(BF16)
            zero = jnp.zeros_like(xdt_b)
            rhs = jnp.concatenate([jnp.where(lane_lo, xdt_b, zero), jnp.where(lane_lo, zero, xdt_b)], axis=0)
            y = _dot(jnp.concatenate(m_parts, axis=1), rhs)
            y = y + y_state[:, pr * LANES:(pr + 1) * LANES] * jnp.exp2(a_pair)
            y_ref[:, ch] = y + x_pair * dskip_ref[:, ch]
            a_last = a_pair[ln - 1:ln, :]
            xw_parts.append((xdt * jnp.exp2(a_last - a_pair)).astype(BF16))
            decay_parts.append(jnp.exp2(a_last))
        xw = jnp.concatenate(xw_parts, axis=1)
        decay = jnp.concatenate(decay_parts, axis=1)
        state_ref[g] = state_ref[g] * decay + _dot_tn(b_g, xw)

    gated = jnp.concatenate([y_ref[:, 0:cbw] * z0_ref[...], y_ref[:, cbw:2 * cbw] * z1_ref[...]], axis=1)
    o_ref[...] = _rms(gated, ng_ref[...]).astype(o_ref.dtype)


def _ssd_scan(z_act, xbc, dt_raw, dt_bias, a_log, d_skip, norm_g):
    s = z_act.shape[0]
    ln = SSD_CHUNK
    cbw = SSD_COL_BLOCK
    assert SSD_INNER == 2 * cbw and 2 * SSD_GROUPS * SSD_STATE == cbw

    def col_block(j):
        return pl.BlockSpec((ln, cbw), lambda c: (c, j))

    def whole(shape):
        return pl.BlockSpec(shape, lambda c: (0,) * len(shape))

    return pl.pallas_call(
        _ssd_kernel,
        grid=(s // ln,),
        in_specs=[
            col_block(0), col_block(1), col_block(0), col_block(1), col_block(2),
            pl.BlockSpec((ln, LANES), lambda c: (c, 0)),
            whole((1, SSD_HEADS)), whole((1, SSD_HEADS)),
            whole((1, SSD_INNER)), whole((1, SSD_INNER)),
        ],
        out_specs=pl.BlockSpec((ln, SSD_INNER), lambda c: (c, 0)),
        out_shape=jax.ShapeDtypeStruct((s, SSD_INNER), BF16),
        scratch_shapes=[
            pltpu.VMEM((ln, 2 * SSD_GROUPS * SSD_STATE), BF16),
            pltpu.VMEM((ln, SSD_INNER), F32),
            pltpu.VMEM((SSD_GROUPS, SSD_STATE, SSD_GROUP_WIDTH), F32),
        ],
        compiler_params=_params("arbitrary"),
        name="ssd_scan",
    )(z_act, z_act, xbc, xbc, xbc, dt_raw,
      dt_bias.reshape(1, SSD_HEADS), a_log.reshape(1, SSD_HEADS),
      jnp.repeat(d_skip, SSD_HEAD_DIM).reshape(1, SSD_INNER), norm_g.reshape(1, SSD_INNER))


def _head_rms(y, gain, n_heads):
    return [_rms(y[:, h * XA_HEAD_DIM:(h + 1) * XA_HEAD_DIM], gain) for h in range(n_heads)]


def _mem_kv_kernel(mem_ref, g_ref, w_ref, kg_ref, k_ref, v_ref):
    xw = XA_HEADS * XA_HEAD_DIM
    kv = _dot(_rms(mem_ref[...], g_ref[...]).astype(BF16), w_ref[...].astype(BF16))
    k_ref[...] = jnp.concatenate(_head_rms(kv[:, 0:xw], kg_ref[...], XA_HEADS), axis=1).astype(BF16)
    v_ref[...] = kv[:, xw:2 * xw].astype(BF16)


def _mem_kv(mem, g, w_kv, layer, k_gain):
    m, d = mem.shape
    xw = XA_HEADS * XA_HEAD_DIM

    def whole(shape):
        return pl.BlockSpec(shape, lambda i: (0,) * len(shape))

    return pl.pallas_call(
        _mem_kv_kernel,
        grid=(1,),
        in_specs=[whole((m, d)), whole((1, d)), pl.BlockSpec((None, d, 2 * xw), lambda i: (layer, 0, 0)),
                  whole((1, XA_HEAD_DIM))],
        out_specs=(whole((m, xw)), whole((m, xw))),
        out_shape=(jax.ShapeDtypeStruct((m, xw), BF16), jax.ShapeDtypeStruct((m, xw), BF16)),
        compiler_params=_params("arbitrary"),
        name="mem_kv",
    )(mem, g.reshape(1, d), w_kv, k_gain.reshape(1, XA_HEAD_DIM))


def _xattn_kernel(x_ref, g_ref, wq_ref, qg_ref, k_ref, v_ref, wo_ref, o_ref, wq_b_ref, wo_b_ref):
    @pl.when(pl.program_id(0) == 0)
    def _():
        wq_b_ref[...] = wq_ref[...].astype(BF16)
        wo_b_ref[...] = wo_ref[...].astype(BF16)

    x = x_ref[...]
    q = _dot(_rms(x, g_ref[...]).astype(BF16), wq_b_ref[...])
    heads = []
    for h, qh in enumerate(_head_rms(q, qg_ref[...], XA_HEADS)):
        cols = slice(h * XA_HEAD_DIM, (h + 1) * XA_HEAD_DIM)
        s = _dot_nt((qh * XA_HEAD_DIM ** -0.5).astype(BF16), k_ref[:, cols])
        p = jnp.exp(s - jnp.max(s, axis=-1, keepdims=True))
        p = p / jnp.sum(p, axis=-1, keepdims=True)
        heads.append(_dot(p.astype(BF16), v_ref[:, cols]))
    o = jnp.concatenate(heads, axis=1).astype(BF16)
    o_ref[...] = x + _dot(o, wo_b_ref[...])


def _xattn(x, g, w_q, q_gain, k, v, w_o, layer, *, tm=512):
    s, d = x.shape
    m, xw = k.shape
    tm = min(tm, s)

    def whole(shape):
        return pl.BlockSpec(shape, lambda i: (0,) * len(shape))

    return pl.pallas_call(
        _xattn_kernel,
        grid=(s // tm,),
        in_specs=[
            pl.BlockSpec((tm, d), lambda i: (i, 0)),
            whole((1, d)), pl.BlockSpec((None, d, xw), lambda i: (layer, 0, 0)), whole((1, XA_HEAD_DIM)),
            whole((m, xw)), whole((m, xw)), pl.BlockSpec((None, xw, d), lambda i: (layer, 0, 0)),
        ],
        out_specs=pl.BlockSpec((tm, d), lambda i: (i, 0)),
        out_shape=jax.ShapeDtypeStruct((s, d), F32),
        scratch_shapes=[pltpu.VMEM((d, xw), BF16), pltpu.VMEM((xw, d), BF16)],
        compiler_params=_params("arbitrary"),
        name="mem_xattn",
    )(x, g.reshape(1, d), w_q, q_gain.reshape(1, XA_HEAD_DIM), k, v, w_o)


def _attn_layer(x, norm_g, w_qkv, w_o, layer, q_gain, k_gain, lam_vecs, subln_gain, rel_bias, bias, lambda_init, t):
    d = x.shape[1]
    n_maps = 2 * DA_HEADS
    q_col_gain = q_gain * (DA_HEAD_DIM ** -0.5 * LOG2E)
    col_gain = jnp.concatenate([jnp.tile(q_col_gain, n_maps), jnp.tile(k_gain, n_maps)]).reshape(1, 2 * d)
    h = _norm(x, norm_g)
    qk = _proj(h, w_qkv, layer, mode="qk_norm", n_cols=2 * d, extra=(col_gain,), out_dtype=BF16)
    v = _proj(h, w_qkv, layer, mode="plain", n_cols=d, first_col=2 * d, out_dtype=BF16)
    logit_bound = (DA_HEAD_DIM * jnp.max(jnp.abs(q_col_gain)) * jnp.max(jnp.abs(k_gain))
                   + LOG2E * jnp.max(jnp.abs(rel_bias - rel_bias[REL_BUCKETS - 1])))
    o = _diff_attention(qk, v, bias, lam_vecs, subln_gain, lambda_init, logit_bound, t=t)
    return _proj(o, w_o, layer, mode="residual", n_cols=d, extra=(x,), out_dtype=F32)


def _ssd_layer(x, norm_g, w_in, w_out, layer, conv_w, conv_b, dt_bias, a_log, d_skip, ssd_norm):
    n_main = 2 * SSD_INNER + 2 * SSD_GROUPS * SSD_STATE
    h = _norm(x, norm_g)
    w_in_t = jnp.swapaxes(w_in, 1, 2)
    z_act = _proj(h, w_in_t, layer, mode="silu", n_cols=SSD_INNER, out_dtype=F32, transposed=True)
    xbc = _proj(h, w_in_t, layer, mode="conv", n_cols=n_main - SSD_INNER, first_col=SSD_INNER,
                extra=(conv_w, conv_b.reshape(1, -1)), out_dtype=F32, transposed=True)
    dt_raw = _proj(h, w_in_t, layer, mode="plain", n_cols=LANES, first_col=n_main, out_dtype=F32, tn=LANES,
                   transposed=True)
    y = _ssd_scan(z_act, xbc, dt_raw, dt_bias, a_log, d_skip, ssd_norm)
    return _proj(y, w_out, layer, mode="residual", n_cols=x.shape[1], extra=(x,), out_dtype=F32, tn=512)


@jax.jit
def kernel(x, mem, rel_bias, ffn1_norm, ffn1_w_gate, ffn1_w_up, ffn1_w_down, mix_norm, attn_w_qkv, attn_w_o,
           attn_q_norm, attn_k_norm, attn_lambda, attn_subln, ssd_w_in, ssd_conv_w, ssd_conv_b, ssd_dt_bias,
           ssd_a_log, ssd_d, ssd_norm, ssd_w_out, xattn_norm, mem_norm, xattn_w_q, xattn_w_kv, xattn_w_o,
           xattn_q_norm, xattn_k_norm, ffn2_norm, ffn2_w_gate, ffn2_w_up, ffn2_w_down):
    b, s, d = x.shape
    assert b == 1
    t = min(ATTN_BLOCK, s)
    xs = x[0]
    mem2 = mem[0]
    bias = _bias_tiles(rel_bias, t)
    for i in range(DEPTH):
        xs = _ffn(xs, ffn1_norm[i], ffn1_w_gate, ffn1_w_up, ffn1_w_down, i)
        j = i // N_MIXERS
        if i % N_MIXERS == 0:
            lambda_init = 0.8 - 0.6 * math.exp(-0.3 * i)
            xs = _attn_layer(xs, mix_norm[i], attn_w_qkv, attn_w_o, j, attn_q_norm[j], attn_k_norm[j],
                             attn_lambda[j], attn_subln[j], rel_bias, bias, lambda_init, t)
        else:
            xs = _ssd_layer(xs, mix_norm[i], ssd_w_in, ssd_w_out, j, ssd_conv_w[j], ssd_conv_b[j], ssd_dt_bias[j],
                            ssd_a_log[j], ssd_d[j], ssd_norm[j])
        k, v = _mem_kv(mem2, mem_norm[i], xattn_w_kv, i, xattn_k_norm[i])
        xs = _xattn(xs, xattn_norm[i], xattn_w_q, xattn_q_norm[i], k, v, xattn_w_o, i)
        xs = _ffn(xs, ffn2_norm[i], ffn2_w_gate, ffn2_w_up, ffn2_w_down, i)
    return xs[None]
```

```python
import functools
import math

import jax
import jax.numpy as jnp
from jax import lax
from jax.experimental import pallas as pl
from jax.experimental.pallas import tpu as pltpu

F32 = jnp.float32
BF16 = jnp.bfloat16

EPS = 1e-6
NEG_INF = -1e30
LOG2E = math.log2(math.e)

DEPTH = 4
N_MIXERS = 2

DA_HEADS = 8
DA_HEAD_DIM = 128
DA_V_DIM = 2 * DA_HEAD_DIM
REL_BUCKETS = 32
REL_MAX_DIST = 128
ATTN_BLOCK = 512
FAR_GROUP = 4
MAX_UNSHIFTED_LOG2_LOGIT = 100.0

SSD_HEAD_DIM = 64
SSD_HEADS = 64
SSD_GROUPS = 8
SSD_STATE = 128
SSD_CONV = 4
SSD_CHUNK = 128
SSD_INNER = SSD_HEADS * SSD_HEAD_DIM
SSD_HEADS_PER_GROUP = SSD_HEADS // SSD_GROUPS
SSD_GROUP_WIDTH = SSD_HEADS_PER_GROUP * SSD_HEAD_DIM
SSD_COL_BLOCK = 2048
CONV_HALO = 8
FFN_FINAL_SUB_ROWS = 128
PROJ_SUB_ROWS = 256

XA_HEADS = 4
XA_HEAD_DIM = 128

LANES = 128
VMEM_LIMIT_BYTES = 56 * 1024 * 1024


def _params(*sem):
    return pltpu.CompilerParams(dimension_semantics=sem, vmem_limit_bytes=VMEM_LIMIT_BYTES)


def _rms(x, g):
    return x * lax.rsqrt(jnp.mean(x * x, axis=-1, keepdims=True) + EPS) * g


def _silu(x):
    half = 0.5 * x
    return half + half * jnp.tanh(half)


def _dot(a, b):
    return jnp.dot(a, b, preferred_element_type=F32)


def _dot_nt(a, b):
    return lax.dot_general(a, b, (((1,), (1,)), ((), ())), preferred_element_type=F32)


def _dot_tn(a, b):
    return lax.dot_general(a, b, (((0,), (0,)), ((), ())), preferred_element_type=F32)


def _ffn_kernel(*refs, emit_next_norm):
    if emit_next_norm:
        x_ref, g_ref, wg_ref, wu_ref, wd_ref, gn_ref, o_ref, h_ref = refs
    else:
        x_ref, g_ref, wg_ref, wu_ref, wd_ref, o_ref, h_ref = refs
    j = pl.program_id(1)

    @pl.when(j == 0)
    def _():
        h_ref[...] = _rms(x_ref[...], g_ref[...]).astype(BF16)
        o_ref[...] = jnp.zeros_like(o_ref)

    h = h_ref[...]
    gate = _dot(h, wg_ref[...].astype(BF16))
    up = _dot(h, wu_ref[...].astype(BF16))
    a = (_silu(gate) * up).astype(BF16)
    o_ref[...] += _dot(a, wd_ref[...].astype(BF16))

    @pl.when(j == pl.num_programs(1) - 1)
    def _():
        if emit_next_norm:
            tm = x_ref.shape[0]
            sub = min(FFN_FINAL_SUB_ROWS, tm)
            for r in range(tm // sub):
                rows = slice(r * sub, (r + 1) * sub)
                out = x_ref[rows, :] + 0.5 * o_ref[rows, :]
                o_ref[rows, :] = out
                h_ref[rows, :] = _rms(out, gn_ref[...]).astype(BF16)
        else:
            o_ref[...] = x_ref[...] + 0.5 * o_ref[...]


def _ffn(x, g, wg, wu, wd, layer, next_gain=None, *, tm=1024, tf=256):
    s, d = x.shape
    f = wg.shape[2]
    tm = min(tm, s)
    emit = next_gain is not None
    row_tile = pl.BlockSpec((tm, d), lambda i, j: (i, 0))
    gain = pl.BlockSpec((1, d), lambda i, j: (0, 0))
    in_specs = [
        row_tile, gain,
        pl.BlockSpec((None, d, tf), lambda i, j: (layer, 0, j)),
        pl.BlockSpec((None, d, tf), lambda i, j: (layer, 0, j)),
        pl.BlockSpec((None, tf, d), lambda i, j: (layer, j, 0)),
    ]
    operands = [x, g.reshape(1, d), wg, wu, wd]
    if emit:
        in_specs.append(gain)
        operands.append(next_gain.reshape(1, d))
    return pl.pallas_call(
        functools.partial(_ffn_kernel, emit_next_norm=emit),
        grid=(s // tm, f // tf),
        in_specs=in_specs,
        out_specs=(row_tile, row_tile) if emit else row_tile,
        out_shape=((jax.ShapeDtypeStruct((s, d), F32), jax.ShapeDtypeStruct((s, d), BF16)) if emit
                   else jax.ShapeDtypeStruct((s, d), F32)),
        scratch_shapes=[] if emit else [pltpu.VMEM((tm, d), BF16)],
        compiler_params=_params("parallel", "arbitrary"),
        name="ffn",
    )(*operands)


def _proj_kernel(*refs, mode, n_valid, transposed):
    if mode == "qk_norm":
        h_ref, w_ref, cg_ref, o_ref, wb_ref = refs
    elif mode == "residual":
        h_ref, w_ref, x_ref, o_ref, wb_ref = refs
    elif mode == "conv":
        h_ref, w_ref, cw_ref, cb_ref, o_ref, wb_ref, halo_ref = refs
    else:
        h_ref, w_ref, o_ref, wb_ref = refs
    j = pl.program_id(0)
    i = pl.program_id(1)

    @pl.when(i == 0)
    def _():
        w = w_ref[...]
        col_axis = 0 if transposed else 1
        if n_valid < w.shape[col_axis]:
            limit = jnp.where(j == pl.num_programs(0) - 1, n_valid, w.shape[col_axis])
            w = jnp.where(lax.broadcasted_iota(jnp.int32, w.shape, col_axis) < limit, w, 0.0)
        wb_ref[...] = w.astype(BF16)
        if mode == "conv":
            halo_ref[...] = jnp.zeros(halo_ref.shape, F32)

    tm = h_ref.shape[0]
    sub = min(PROJ_SUB_ROWS, tm)
    halo = halo_ref[...] if mode == "conv" else None
    for r in range(tm // sub):
        rows = slice(r * sub, (r + 1) * sub)
        y = (_dot_nt if transposed else _dot)(h_ref[rows, :], wb_ref[...])
        if mode == "qk_norm":
            for c in range(y.shape[1] // LANES):
                cols = slice(c * LANES, (c + 1) * LANES)
                o_ref[rows, cols] = _rms(y[:, cols], cg_ref[:, cols]).astype(o_ref.dtype)
        elif mode == "silu":
            o_ref[rows, :] = _silu(y).astype(o_ref.dtype)
        elif mode == "residual":
            o_ref[rows, :] = x_ref[rows, :] + y
        elif mode == "conv":
            ext = jnp.concatenate([halo, y], axis=0)
            acc = cb_ref[...] + cw_ref[SSD_CONV - 1:SSD_CONV, :] * y
            for k in range(SSD_CONV - 1):
                shifted = pltpu.roll(ext, SSD_CONV - 1 - k, axis=0)[CONV_HALO:, :]
                acc = acc + cw_ref[k:k + 1, :] * shifted
            halo = y[sub - CONV_HALO:, :]
            o_ref[rows, :] = _silu(acc).astype(o_ref.dtype)
        else:
            o_ref[rows, :] = y.astype(o_ref.dtype)
    if mode == "conv":
        halo_ref[...] = halo


def _proj(h, w, layer, *, mode, n_cols, first_col=0, extra=(), out_dtype, transposed=False, tm=1024, tn=1024):
    s, d = h.shape
    tm = min(tm, s)
    tn = min(tn, n_cols)
    assert n_cols % tn == 0 and first_col % tn == 0
    col0 = first_col // tn
    w_cols = w.shape[1] if transposed else w.shape[2]
    n_valid = min(tn, w_cols - first_col - (n_cols - tn))
    if transposed:
        w_spec = pl.BlockSpec((None, tn, d), lambda j, i: (layer, col0 + j, 0))
    else:
        w_spec = pl.BlockSpec((None, d, tn), lambda j, i: (layer, 0, col0 + j))
    in_specs = [pl.BlockSpec((tm, d), lambda j, i: (i, 0)), w_spec]
    if mode == "residual":
        in_specs.append(pl.BlockSpec((tm, tn), lambda j, i: (i, j)))
    else:
        in_specs += [pl.BlockSpec((e.shape[0], tn), lambda j, i: (0, j)) for e in extra]
    scratch = [pltpu.VMEM((tn, d) if transposed else (d, tn), BF16)]
    if mode == "conv":
        scratch.append(pltpu.VMEM((CONV_HALO, tn), F32))
    return pl.pallas_call(
        functools.partial(_proj_kernel, mode=mode, n_valid=n_valid, transposed=transposed),
        grid=(n_cols // tn, s // tm),
        in_specs=in_specs,
        out_specs=pl.BlockSpec((tm, tn), lambda j, i: (i, j)),
        out_shape=jax.ShapeDtypeStruct((s, n_cols), out_dtype),
        scratch_shapes=scratch,
        compiler_params=_params("arbitrary", "arbitrary"),
        name="proj_" + mode,
    )(h, w, *extra)


def _bias_kernel(tab_ref, o_ref, *, t):
    h = pl.program_id(0)
    row = lax.broadcasted_iota(jnp.int32, (t, t), 0)
    col = lax.broadcasted_iota(jnp.int32, (t, t), 1)
    max_exact = REL_BUCKETS // 2
    far = tab_ref[REL_BUCKETS - 1, h]
    for blk in range(2):
        dist = row - col + blk * t
        n = jnp.maximum(dist, 0)
        nf = jnp.maximum(n, 1).astype(F32)
        large = max_exact + (jnp.log(nf / max_exact) / math.log(REL_MAX_DIST / max_exact)
                             * (REL_BUCKETS - max_exact)).astype(jnp.int32)
        large = jnp.minimum(large, REL_BUCKETS - 1)
        bucket = jnp.where(n < max_exact, n, large)
        bias = jnp.zeros((t, t), F32)
        for b in range(REL_BUCKETS - 1):
            bias = jnp.where(bucket == b, (tab_ref[b, h] - far) * LOG2E, bias)
        if blk == 0:
            bias = jnp.where(dist >= 0, bias, NEG_INF)
        o_ref[0, :, (1 - blk) * t:(2 - blk) * t] = bias


def _bias_tiles(rel_bias, t):
    assert t >= REL_MAX_DIST
    return pl.pallas_call(
        functools.partial(_bias_kernel, t=t),
        grid=(DA_HEADS,),
        in_specs=[pl.BlockSpec(memory_space=pltpu.SMEM)],
        out_specs=pl.BlockSpec((1, t, 2 * t), lambda h: (h, 0, 0)),
        out_shape=jax.ShapeDtypeStruct((DA_HEADS, t, 2 * t), F32),
        compiler_params=_params("arbitrary"),
        name="rel_bias_tiles",
    )(rel_bias)


def _attn_kernel(q_ref, k_ref, v_ref, bias_ref, lv_ref, sg_ref, o_ref, m_ref, l_ref, acc_ref, *, t, lambda_init,
                 online):
    qb = pl.program_id(1)
    dh = DA_HEAD_DIM
    if online:
        m_ref[...] = jnp.full(m_ref.shape, NEG_INF, F32)
    l_ref[...] = jnp.zeros(l_ref.shape, F32)
    acc_ref[...] = jnp.zeros(acc_ref.shape, F32)

    def lane_chunk_sum(p):
        out = p[:, 0:LANES]
        for c in range(1, p.shape[1] // LANES):
            out = out + p[:, c * LANES:(c + 1) * LANES]
        return out

    def block(start, width, bias):
        v = v_ref[pl.ds(start, width), :]
        for mp in range(2):
            cols = slice(mp * dh, (mp + 1) * dh)
            s = _dot_nt(q_ref[:, cols], k_ref[pl.ds(start, width), cols])
            if bias is not None:
                plain = width - bias.shape[1]
                biased = s[:, plain:] + bias
                s = biased if plain == 0 else jnp.concatenate([s[:, :plain], biased], axis=1)
            if online:
                m_prev = m_ref[mp]
                m_new = jnp.maximum(m_prev, jnp.max(s, axis=-1, keepdims=True))
                alpha = jnp.exp2(m_prev - m_new)
                p = jnp.exp2(s - m_new)
                l_ref[mp] = alpha * l_ref[mp] + lane_chunk_sum(p)
                acc_ref[mp] = alpha * acc_ref[mp] + _dot(p.astype(BF16), v)
                m_ref[mp] = m_new
            else:
                p = jnp.exp2(s)
                l_ref[mp] += lane_chunk_sum(p)
                acc_ref[mp] += _dot(p.astype(BF16), v)

    n_far = jnp.maximum(qb - 1, 0)
    n_groups = n_far // FAR_GROUP
    n_left = n_far - n_groups * FAR_GROUP

    def far_group(i, carry):
        block(pl.multiple_of(i * (FAR_GROUP * t), FAR_GROUP * t), FAR_GROUP * t, None)
        return carry

    lax.fori_loop(0, n_groups, far_group, 0)

    for left in range(FAR_GROUP):
        @pl.when(jnp.logical_and(qb > 0, n_left == left))
        def _(left=left):
            block(pl.multiple_of((qb - 1 - left) * t, t), (left + 2) * t, bias_ref[0])

    @pl.when(qb == 0)
    def _():
        block(0, t, bias_ref[0, :, t:2 * t])

    lv = lv_ref[...]
    lam = (jnp.exp(jnp.sum(lv[0:1] * lv[1:2], axis=-1, keepdims=True))
           - jnp.exp(jnp.sum(lv[2:3] * lv[3:4], axis=-1, keepdims=True)) + lambda_init)
    l1 = jnp.sum(l_ref[0], axis=-1, keepdims=True)
    l2 = jnp.sum(l_ref[1], axis=-1, keepdims=True)
    o = acc_ref[0] / l1 - lam * (acc_ref[1] / l2)
    o_ref[...] = (_rms(o, sg_ref[...]) * (1.0 - lambda_init)).astype(o_ref.dtype)


def _diff_attention(qk, v, bias, lam_vecs, subln_gain, lambda_init, logit_bound, *, t):
    s = qk.shape[0]
    t = min(t, s)
    assert s >= (FAR_GROUP + 1) * t
    hw = DA_V_DIM

    def call(online):
        return pl.pallas_call(
            functools.partial(_attn_kernel, t=t, lambda_init=lambda_init, online=online),
            grid=(DA_HEADS, s // t),
            in_specs=[
                pl.BlockSpec((t, hw), lambda h, i: (i, h)),
                pl.BlockSpec((s, hw), lambda h, i: (0, DA_HEADS + h)),
                pl.BlockSpec((s, hw), lambda h, i: (0, h)),
                pl.BlockSpec((1, t, 2 * t), lambda h, i: (h, 0, 0)),
                pl.BlockSpec((4, DA_HEAD_DIM), lambda h, i: (0, 0)),
                pl.BlockSpec((1, hw), lambda h, i: (0, 0)),
            ],
            out_specs=pl.BlockSpec((t, hw), lambda h, i: (i, h)),
            out_shape=jax.ShapeDtypeStruct((s, DA_HEADS * hw), BF16),
            scratch_shapes=[
                pltpu.VMEM((2, t, 1), F32),
                pltpu.VMEM((2, t, LANES), F32),
                pltpu.VMEM((2, t, hw), F32),
            ],
            compiler_params=_params("parallel", "arbitrary"),
            name="diff_attn_online" if online else "diff_attn",
        )

    operands = (qk, qk, v, bias, lam_vecs, subln_gain.reshape(1, hw))
    return lax.cond(logit_bound < MAX_UNSHIFTED_LOG2_LOGIT,
                    lambda *a: call(False)(*a), lambda *a: call(True)(*a), *operands)


def _ssd_kernel(z0_ref, z1_ref, x0_ref, x1_ref, bc_ref, dt_ref, dtb_ref, alog_ref, dskip_ref,
                ng_ref, o_ref, bcs_ref, y_ref, state_ref):
    c = pl.program_id(0)
    ln = SSD_CHUNK
    cbw = SSD_COL_BLOCK
    ns = SSD_STATE

    @pl.when(c == 0)
    def _():
        state_ref[...] = jnp.zeros(state_ref.shape, F32)

    bcs_ref[...] = bc_ref[...].astype(BF16)
    x_refs = (x0_ref, x1_ref)

    dt = jax.nn.softplus(dt_ref[:, 0:SSD_HEADS] + dtb_ref[...])
    a = dt * (-LOG2E * jnp.exp(alog_ref[...]))
    row = lax.broadcasted_iota(jnp.int32, (ln, ln), 0)
    col = lax.broadcasted_iota(jnp.int32, (ln, ln), 1)
    causal = row >= col
    acum = jnp.dot(causal.astype(F32), a, preferred_element_type=F32, precision=lax.Precision.HIGHEST)
    acum_t = acum.T
    lane_lo = lax.broadcasted_iota(jnp.int32, (ln, LANES), 1) < SSD_HEAD_DIM

    for g in range(SSD_GROUPS):
        b_g = bcs_ref[:, g * ns:(g + 1) * ns]
        c_g = bcs_ref[:, SSD_GROUPS * ns + g * ns:SSD_GROUPS * ns + (g + 1) * ns]
        cb = _dot_nt(c_g, b_g)
        y_state = _dot(c_g, state_ref[g].astype(BF16))
        xw_parts = []
        decay_parts = []
        for pr in range(SSD_HEADS_PER_GROUP // 2):
            h0 = g * SSD_HEADS_PER_GROUP + 2 * pr
            ch = slice(h0 * SSD_HEAD_DIM, h0 * SSD_HEAD_DIM + LANES)
            m_parts = []
            a_cols = []
            for h in (h0, h0 + 1):
                a_col = jnp.broadcast_to(acum[:, h:h + 1], (ln, LANES))
                seg = a_col - acum_t[h:h + 1, :]
                m_parts.append((cb * jnp.where(causal, jnp.exp2(seg), 0.0)).astype(BF16))
                a_cols.append(a_col)
            a_pair = jnp.where(lane_lo, a_cols[0], a_cols[1])
            dt_pair = jnp.where(lane_lo, dt[:, h0:h0 + 1], dt[:, h0 + 1:h0 + 2])
            x_pair = x_refs[ch.start // cbw][:, ch.start % cbw:ch.start % cbw + LANES]
            xdt = x_pair * dt_pair
            xdt_b = xdt.astype(BF16)
            zero = jnp.zeros_like(xdt_b)
            rhs = jnp.concatenate([jnp.where(lane_lo, xdt_b, zero), jnp.where(lane_lo, zero, xdt_b)], axis=0)
            y = _dot(jnp.concatenate(m_parts, axis=1), rhs)
            y = y + y_state[:, pr * LANES:(pr + 1) * LANES] * jnp.exp2(a_pair)
            y_ref[:, ch] = y + x_pair * dskip_ref[:, ch]
            a_last = a_pair[ln - 1:ln, :]
            xw_parts.append((xdt * jnp.exp2(a_last - a_pair)).astype(BF16))
            decay_parts.append(jnp.exp2(a_last))
        xw = jnp.concatenate(xw_parts, axis=1)
        decay = jnp.concatenate(decay_parts, axis=1)
        state_ref[g] = state_ref[g] * decay + _dot_tn(b_g, xw)

    gated = jnp.concatenate([y_ref[:, 0:cbw] * z0_ref[...], y_ref[:, cbw:2 * cbw] * z1_ref[...]], axis=1)
    o_ref[...] = _rms(gated, ng_ref[...]).astype(o_ref.dtype)


def _ssd_scan(z_act, xbc, dt_raw, dt_bias, a_log, d_skip, norm_g):
    s = z_act.shape[0]
    ln = SSD_CHUNK
    cbw = SSD_COL_BLOCK
    assert SSD_INNER == 2 * cbw and 2 * SSD_GROUPS * SSD_STATE == cbw

    def col_block(j):
        return pl.BlockSpec((ln, cbw), lambda c: (c, j))

    def whole(shape):
        return pl.BlockSpec(shape, lambda c: (0,) * len(shape))

    return pl.pallas_call(
        _ssd_kernel,
        grid=(s // ln,),
        in_specs=[
            col_block(0), col_block(1), col_block(0), col_block(1), col_block(2),
            pl.BlockSpec((ln, LANES), lambda c: (c, 0)),
            whole((1, SSD_HEADS)), whole((1, SSD_HEADS)),
            whole((1, SSD_INNER)), whole((1, SSD_INNER)),
        ],
        out_specs=pl.BlockSpec((ln, SSD_INNER), lambda c: (c, 0)),
        out_shape=jax.ShapeDtypeStruct((s, SSD_INNER), BF16),
        scratch_shapes=[
            pltpu.VMEM((ln, 2 * SSD_GROUPS * SSD_STATE), BF16),
            pltpu.VMEM((ln, SSD_INNER), F32),
            pltpu.VMEM((SSD_GROUPS, SSD_STATE, SSD_GROUP_WIDTH), F32),
        ],
        compiler_params=_params("arbitrary"),
        name="ssd_scan",
    )(z_act, z_act, xbc, xbc, xbc, dt_raw,
      dt_bias.reshape(1, SSD_HEADS), a_log.reshape(1, SSD_HEADS),
      jnp.repeat(d_skip, SSD_HEAD_DIM).reshape(1, SSD_INNER), norm_g.reshape(1, SSD_INNER))


def _head_rms(y, gain, n_heads):
    return [_rms(y[:, h * XA_HEAD_DIM:(h + 1) * XA_HEAD_DIM], gain) for h in range(n_heads)]


def _mem_kv_kernel(mem_ref, g_ref, w_ref, kg_ref, k_ref, v_ref):
    xw = XA_HEADS * XA_HEAD_DIM
    kv = _dot(_rms(mem_ref[...], g_ref[...]).astype(BF16), w_ref[...].astype(BF16))
    k_ref[...] = jnp.concatenate(_head_rms(kv[:, 0:xw], kg_ref[...], XA_HEADS), axis=1).astype(BF16)
    v_ref[...] = kv[:, xw:2 * xw].astype(BF16)


def _mem_kv(mem, g, w_kv, layer, k_gain):
    m, d = mem.shape
    xw = XA_HEADS * XA_HEAD_DIM

    def whole(shape):
        return pl.BlockSpec(shape, lambda i: (0,) * len(shape))

    return pl.pallas_call(
        _mem_kv_kernel,
        grid=(1,),
        in_specs=[whole((m, d)), whole((1, d)), pl.BlockSpec((None, d, 2 * xw), lambda i: (layer, 0, 0)),
                  whole((1, XA_HEAD_DIM))],
        out_specs=(whole((m, xw)), whole((m, xw))),
        out_shape=(jax.ShapeDtypeStruct((m, xw), BF16), jax.ShapeDtypeStruct((m, xw), BF16)),
        compiler_params=_params("arbitrary"),
        name="mem_kv",
    )(mem, g.reshape(1, d), w_kv, k_gain.reshape(1, XA_HEAD_DIM))


def _xattn_kernel(x_ref, g_ref, wq_ref, qg_ref, k_ref, v_ref, wo_ref, o_ref, wq_b_ref, wo_b_ref):
    @pl.when(pl.program_id(0) == 0)
    def _():
        wq_b_ref[...] = wq_ref[...].astype(BF16)
        wo_b_ref[...] = wo_ref[...].astype(BF16)

    x = x_ref[...]
    q = _dot(_rms(x, g_ref[...]).astype(BF16), wq_b_ref[...])
    heads = []
    for h, qh in enumerate(_head_rms(q, qg_ref[...], XA_HEADS)):
        cols = slice(h * XA_HEAD_DIM, (h + 1) * XA_HEAD_DIM)
        s = _dot_nt((qh * XA_HEAD_DIM ** -0.5).astype(BF16), k_ref[:, cols])
        p = jnp.exp(s - jnp.max(s, axis=-1, keepdims=True))
        p = p / jnp.sum(p, axis=-1, keepdims=True)
        heads.append(_dot(p.astype(BF16), v_ref[:, cols]))
    o = jnp.concatenate(heads, axis=1).astype(BF16)
    o_ref[...] = x + _dot(o, wo_b_ref[...])


def _xattn(x, g, w_q, q_gain, k, v, w_o, layer, *, tm=512):
    s, d = x.shape
    m, xw = k.shape
    tm = min(tm, s)

    def whole(shape):
        return pl.BlockSpec(shape, lambda i: (0,) * len(shape))

    return pl.pallas_call(
        _xattn_kernel,
        grid=(s // tm,),
        in_specs=[
            pl.BlockSpec((tm, d), lambda i: (i, 0)),
            whole((1, d)), pl.BlockSpec((None, d, xw), lambda i: (layer, 0, 0)), whole((1, XA_HEAD_DIM)),
            whole((m, xw)), whole((m, xw)), pl.BlockSpec((None, xw, d), lambda i: (layer, 0, 0)),
        ],
        out_specs=pl.BlockSpec((tm, d), lambda i: (i, 0)),
        out_shape=jax.ShapeDtypeStruct((s, d), F32),
        scratch_shapes=[pltpu.VMEM((d, xw), BF16), pltpu.VMEM((xw, d), BF16)],
        compiler_params=_params("arbitrary"),
        name="mem_xattn",
    )(x, g.reshape(1, d), w_q, q_gain.reshape(1, XA_HEAD_DIM), k, v, w_o)


def _attn_layer(x, h, w_qkv, w_o, layer, q_gain, k_gain, lam_vecs, subln_gain, rel_bias, bias, lambda_init, t):
    d = x.shape[1]
    n_maps = 2 * DA_HEADS
    q_col_gain = q_gain * (DA_HEAD_DIM ** -0.5 * LOG2E)
    col_gain = jnp.concatenate([jnp.tile(q_col_gain, n_maps), jnp.tile(k_gain, n_maps)]).reshape(1, 2 * d)
    qk = _proj(h, w_qkv, layer, mode="qk_norm", n_cols=2 * d, extra=(col_gain,), out_dtype=BF16)
    v = _proj(h, w_qkv, layer, mode="plain", n_cols=d, first_col=2 * d, out_dtype=BF16)
    logit_bound = (DA_HEAD_DIM * jnp.max(jnp.abs(q_col_gain)) * jnp.max(jnp.abs(k_gain))
                   + LOG2E * jnp.max(jnp.abs(rel_bias - rel_bias[REL_BUCKETS - 1])))
    o = _diff_attention(qk, v, bias, lam_vecs, subln_gain, lambda_init, logit_bound, t=t)
    return _proj(o, w_o, layer, mode="residual", n_cols=d, extra=(x,), out_dtype=F32)


def _ssd_layer(x, h, w_in, w_out, layer, conv_w, conv_b, dt_bias, a_log, d_skip, ssd_norm):
    n_main = 2 * SSD_INNER + 2 * SSD_GROUPS * SSD_STATE
    w_in_t = jnp.swapaxes(w_in, 1, 2)
    z_act = _proj(h, w_in_t, layer, mode="silu", n_cols=SSD_INNER, out_dtype=F32, transposed=True)
    xbc = _proj(h, w_in_t, layer, mode="conv", n_cols=n_main - SSD_INNER, first_col=SSD_INNER,
                extra=(conv_w, conv_b.reshape(1, -1)), out_dtype=F32, transposed=True)
    dt_raw = _proj(h, w_in_t, layer, mode="plain", n_cols=LANES, first_col=n_main, out_dtype=F32, tn=LANES,
                   transposed=True)
    y = _ssd_scan(z_act, xbc, dt_raw, dt_bias, a_log, d_skip, ssd_norm)
    return _proj(y, w_out, layer, mode="residual", n_cols=x.shape[1], extra=(x,), out_dtype=F32, tn=512)


@jax.jit
def kernel(x, mem, rel_bias, ffn1_norm, ffn1_w_gate, ffn1_w_up, ffn1_w_down, mix_norm, attn_w_qkv, attn_w_o,
           attn_q_norm, attn_k_norm, attn_lambda, attn_subln, ssd_w_in, ssd_conv_w, ssd_conv_b, ssd_dt_bias,
           ssd_a_log, ssd_d, ssd_norm, ssd_w_out, xattn_norm, mem_norm, xattn_w_q, xattn_w_kv, xattn_w_o,
           xattn_q_norm, xattn_k_norm, ffn2_norm, ffn2_w_gate, ffn2_w_up, ffn2_w_down):
    b, s, d = x.shape
    assert b == 1
    t = min(ATTN_BLOCK, s)
    xs = x[0]
    mem2 = mem[0]
    bias = _bias_tiles(rel_bias, t)
    for i in range(DEPTH):
        xs, h_mix = _ffn(xs, ffn1_norm[i], ffn1_w_gate, ffn1_w_up, ffn1_w_down, i, next_gain=mix_norm[i])
        j = i // N_MIXERS
        if i % N_MIXERS == 0:
            lambda_init = 0.8 - 0.6 * math.exp(-0.3 * i)
            xs = _attn_layer(xs, h_mix, attn_w_qkv, attn_w_o, j, attn_q_norm[j], attn_k_norm[j],
                             attn_lambda[j], attn_subln[j], rel_bias, bias, lambda_init, t)
        else:
            xs = _ssd_layer(xs, h_mix, ssd_w_in, ssd_w_out, j, ssd_conv_w[j], ssd_conv_b[j], ssd_dt_bias[j],
                            ssd_a_log[j], ssd_d[j], ssd_norm[j])
        k, v = _mem_kv(mem2, mem_norm[i], xattn_w_kv, i, xattn_k_norm[i])
        xs = _xattn(xs, xattn_norm[i], xattn_w_q, xattn_q_norm[i], k, v, xattn_w_o, i)
        xs = _ffn(xs, ffn2_norm[i], ffn2_w_gate, ffn2_w_up, ffn2_w_down, i)
    return xs[None]
```

```python
import functools
import math

import jax
import jax.numpy as jnp
from jax import lax
from jax.experimental import pallas as pl
from jax.experimental.pallas import tpu as pltpu

F32 = jnp.float32
BF16 = jnp.bfloat16

EPS = 1e-6
NEG_INF = -1e30
LOG2E = math.log2(math.e)

DEPTH = 4
N_MIXERS = 2

DA_HEADS = 8
DA_HEAD_DIM = 128
DA_V_DIM = 2 * DA_HEAD_DIM
REL_BUCKETS = 32
REL_MAX_DIST = 128
ATTN_BLOCK = 512
FAR_GROUP = 4
MAX_UNSHIFTED_LOG2_LOGIT = 100.0

SSD_HEAD_DIM = 64
SSD_HEADS = 64
SSD_GROUPS = 8
SSD_STATE = 128
SSD_CONV = 4
SSD_CHUNK = 128
SSD_INNER = SSD_HEADS * SSD_HEAD_DIM
SSD_HEADS_PER_GROUP = SSD_HEADS // SSD_GROUPS
SSD_GROUP_WIDTH = SSD_HEADS_PER_GROUP * SSD_HEAD_DIM
SSD_COL_BLOCK = 2048
CONV_HALO = 8
FFN_FINAL_SUB_ROWS = 128
PROJ_SUB_ROWS = 256

XA_HEADS = 4
XA_HEAD_DIM = 128

LANES = 128
VMEM_LIMIT_BYTES = 56 * 1024 * 1024


def _params(*sem):
    return pltpu.CompilerParams(dimension_semantics=sem, vmem_limit_bytes=VMEM_LIMIT_BYTES)


def _rms(x, g):
    return x * lax.rsqrt(jnp.mean(x * x, axis=-1, keepdims=True) + EPS) * g


def _silu(x):
    half = 0.5 * x
    return half + half * jnp.tanh(half)


def _dot(a, b):
    return jnp.dot(a, b, preferred_element_type=F32)


def _dot_nt(a, b):
    return lax.dot_general(a, b, (((1,), (1,)), ((), ())), preferred_element_type=F32)


def _dot_tn(a, b):
    return lax.dot_general(a, b, (((0,), (0,)), ((), ())), preferred_element_type=F32)


def _ffn_kernel(*refs, emit_next_norm):
    if emit_next_norm:
        x_ref, g_ref, wg_ref, wu_ref, wd_ref, gn_ref, o_ref, h_ref = refs
    else:
        x_ref, g_ref, wg_ref, wu_ref, wd_ref, o_ref, h_ref = refs
    j = pl.program_id(1)

    @pl.when(j == 0)
    def _():
        h_ref[...] = _rms(x_ref[...], g_ref[...]).astype(BF16)
        o_ref[...] = jnp.zeros_like(o_ref)

    h = h_ref[...]
    gate = _dot(h, wg_ref[...].astype(BF16))
    up = _dot(h, wu_ref[...].astype(BF16))
    a = (_silu(gate) * up).astype(BF16)
    o_ref[...] += _dot(a, wd_ref[...].astype(BF16))

    @pl.when(j == pl.num_programs(1) - 1)
    def _():
        if emit_next_norm:
            tm = x_ref.shape[0]
            sub = min(FFN_FINAL_SUB_ROWS, tm)
            for r in range(tm // sub):
                rows = slice(r * sub, (r + 1) * sub)
                out = x_ref[rows, :] + 0.5 * o_ref[rows, :]
                o_ref[rows, :] = out
                h_ref[rows, :] = _rms(out, gn_ref[...]).astype(BF16)
        else:
            o_ref[...] = x_ref[...] + 0.5 * o_ref[...]


def _ffn(x, g, wg, wu, wd, layer, next_gain=None, *, tm=1024, tf=256):
    s, d = x.shape
    f = wg.shape[2]
    tm = min(tm, s)
    emit = next_gain is not None
    row_tile = pl.BlockSpec((tm, d), lambda i, j: (i, 0))
    gain = pl.BlockSpec((1, d), lambda i, j: (0, 0))
    in_specs = [
        row_tile, gain,
        pl.BlockSpec((None, d, tf), lambda i, j: (layer, 0, j)),
        pl.BlockSpec((None, d, tf), lambda i, j: (layer, 0, j)),
        pl.BlockSpec((None, tf, d), lambda i, j: (layer, j, 0)),
    ]
    operands = [x, g.reshape(1, d), wg, wu, wd]
    if emit:
        in_specs.append(gain)
        operands.append(next_gain.reshape(1, d))
    return pl.pallas_call(
        functools.partial(_ffn_kernel, emit_next_norm=emit),
        grid=(s // tm, f // tf),
        in_specs=in_specs,
        out_specs=(row_tile, row_tile) if emit else row_tile,
        out_shape=((jax.ShapeDtypeStruct((s, d), F32), jax.ShapeDtypeStruct((s, d), BF16)) if emit
                   else jax.ShapeDtypeStruct((s, d), F32)),
        scratch_shapes=[] if emit else [pltpu.VMEM((tm, d), BF16)],
        compiler_params=_params("parallel", "arbitrary"),
        name="ffn",
    )(*operands)


def _proj_kernel(*refs, mode, n_valid, transposed):
    if mode == "qk_norm":
        h_ref, w_ref, cg_ref, o_ref, wb_ref = refs
    elif mode == "residual":
        h_ref, w_ref, x_ref, o_ref, wb_ref = refs
    elif mode == "conv":
        h_ref, w_ref, cw_ref, cb_ref, o_ref, wb_ref, halo_ref = refs
    else:
        h_ref, w_ref, o_ref, wb_ref = refs
    j = pl.program_id(0)
    i = pl.program_id(1)

    @pl.when(i == 0)
    def _():
        w = w_ref[...]
        col_axis = 0 if transposed else 1
        if n_valid < w.shape[col_axis]:
            limit = jnp.where(j == pl.num_programs(0) - 1, n_valid, w.shape[col_axis])
            w = jnp.where(lax.broadcasted_iota(jnp.int32, w.shape, col_axis) < limit, w, 0.0)
        wb_ref[...] = w.astype(BF16)
        if mode == "conv":
            halo_ref[...] = jnp.zeros(halo_ref.shape, F32)

    tm = h_ref.shape[0]
    sub = min(PROJ_SUB_ROWS, tm)
    halo = halo_ref[...] if mode == "conv" else None
    for r in range(tm // sub):
        rows = slice(r * sub, (r + 1) * sub)
        y = (_dot_nt if transposed else _dot)(h_ref[rows, :], wb_ref[...])
        if mode == "qk_norm":
            for c in range(y.shape[1] // LANES):
                cols = slice(c * LANES, (c + 1) * LANES)
                o_ref[rows, cols] = _rms(y[:, cols], cg_ref[:, cols]).astype(o_ref.dtype)
        elif mode == "silu":
            o_ref[rows, :] = _silu(y).astype(o_ref.dtype)
        elif mode == "residual":
            o_ref[rows, :] = x_ref[rows, :] + y
        elif mode == "conv":
            ext = jnp.concatenate([halo, y], axis=0)
            acc = cb_ref[...] + cw_ref[SSD_CONV - 1:SSD_CONV, :] * y
            for k in range(SSD_CONV - 1):
                shifted = pltpu.roll(ext, SSD_CONV - 1 - k, axis=0)[CONV_HALO:, :]
                acc = acc + cw_ref[k:k + 1, :] * shifted
            halo = y[sub - CONV_HALO:, :]
            o_ref[rows, :] = _silu(acc).astype(o_ref.dtype)
        else:
            o_ref[rows, :] = y.astype(o_ref.dtype)
    if mode == "conv":
        halo_ref[...] = halo


def _proj(h, w, layer, *, mode, n_cols, first_col=0, extra=(), out_dtype, transposed=False, tm=1024, tn=1024):
    s, d = h.shape
    tm = min(tm, s)
    tn = min(tn, n_cols)
    assert n_cols % tn == 0 and first_col % tn == 0
    col0 = first_col // tn
    w_cols = w.shape[1] if transposed else w.shape[2]
    n_valid = min(tn, w_cols - first_col - (n_cols - tn))
    if transposed:
        w_spec = pl.BlockSpec((None, tn, d), lambda j, i: (layer, col0 + j, 0))
    else:
        w_spec = pl.BlockSpec((None, d, tn), lambda j, i: (layer, 0, col0 + j))
    in_specs = [pl.BlockSpec((tm, d), lambda j, i: (i, 0)), w_spec]
    if mode == "residual":
        in_specs.append(pl.BlockSpec((tm, tn), lambda j, i: (i, j)))
    else:
        in_specs += [pl.BlockSpec((e.shape[0], tn), lambda j, i: (0, j)) for e in extra]
    scratch = [pltpu.VMEM((tn, d) if transposed else (d, tn), BF16)]
    if mode == "conv":
        scratch.append(pltpu.VMEM((CONV_HALO, tn), F32))
    return pl.pallas_call(
        functools.partial(_proj_kernel, mode=mode, n_valid=n_valid, transposed=transposed),
        grid=(n_cols // tn, s // tm),
        in_specs=in_specs,
        out_specs=pl.BlockSpec((tm, tn), lambda j, i: (i, j)),
        out_shape=jax.ShapeDtypeStruct((s, n_cols), out_dtype),
        scratch_shapes=scratch,
        compiler_params=_params("arbitrary", "arbitrary"),
        name="proj_" + mode,
    )(h, w, *extra)


def _bias_kernel(tab_ref, o_ref, *, t):
    h = pl.program_id(0)
    sb = LANES
    row = lax.broadcasted_iota(jnp.int32, (sb, sb), 0)
    col = lax.broadcasted_iota(jnp.int32, (sb, sb), 1)
    max_exact = REL_BUCKETS // 2
    far = tab_ref[REL_BUCKETS - 1, h]

    def sub_block(offset):
        dist = row - col + offset
        n = jnp.maximum(dist, 0)
        nf = jnp.maximum(n, 1).astype(F32)
        large = max_exact + (jnp.log(nf / max_exact) / math.log(REL_MAX_DIST / max_exact)
                             * (REL_BUCKETS - max_exact)).astype(jnp.int32)
        large = jnp.minimum(large, REL_BUCKETS - 1)
        bucket = jnp.where(n < max_exact, n, large)
        bias = jnp.zeros((sb, sb), F32)
        for b in range(REL_BUCKETS - 1):
            bias = jnp.where(bucket == b, (tab_ref[b, h] - far) * LOG2E, bias)
        return jnp.where(dist >= 0, bias, NEG_INF)

    for blk in range(2):
        for i in range(t // sb):
            for j in range(t // sb):
                offset = blk * t + (i - j) * sb
                if offset - (sb - 1) >= REL_MAX_DIST:
                    tile = jnp.zeros((sb, sb), F32)
                elif offset + (sb - 1) < 0:
                    tile = jnp.full((sb, sb), NEG_INF, F32)
                else:
                    tile = sub_block(offset)
                o_ref[0, i * sb:(i + 1) * sb, (1 - blk) * t + j * sb:(1 - blk) * t + (j + 1) * sb] = tile


def _bias_tiles(rel_bias, t):
    assert t >= REL_MAX_DIST
    return pl.pallas_call(
        functools.partial(_bias_kernel, t=t),
        grid=(DA_HEADS,),
        in_specs=[pl.BlockSpec(memory_space=pltpu.SMEM)],
        out_specs=pl.BlockSpec((1, t, 2 * t), lambda h: (h, 0, 0)),
        out_shape=jax.ShapeDtypeStruct((DA_HEADS, t, 2 * t), F32),
        compiler_params=_params("arbitrary"),
        name="rel_bias_tiles",
    )(rel_bias)


def _attn_kernel(bound_ref, *refs, t, lambda_init):
    small_logits = bound_ref[0] < MAX_UNSHIFTED_LOG2_LOGIT

    @pl.when(small_logits)
    def _():
        _attn_body(*refs, t=t, lambda_init=lambda_init, online=False)

    @pl.when(jnp.logical_not(small_logits))
    def _():
        _attn_body(*refs, t=t, lambda_init=lambda_init, online=True)


def _attn_body(q_ref, k_ref, v_ref, bias_ref, lv_ref, sg_ref, o_ref, m_ref, l_ref, acc_ref, *, t, lambda_init, online):
    qb = pl.program_id(1)
    dh = DA_HEAD_DIM
    if online:
        m_ref[...] = jnp.full(m_ref.shape, NEG_INF, F32)
    l_ref[...] = jnp.zeros(l_ref.shape, F32)
    acc_ref[...] = jnp.zeros(acc_ref.shape, F32)

    def lane_chunk_sum(p):
        out = p[:, 0:LANES]
        for c in range(1, p.shape[1] // LANES):
            out = out + p[:, c * LANES:(c + 1) * LANES]
        return out

    def block(start, width, bias):
        v = v_ref[pl.ds(start, width), :]
        for mp in range(2):
            cols = slice(mp * dh, (mp + 1) * dh)
            s = _dot_nt(q_ref[:, cols], k_ref[pl.ds(start, width), cols])
            if bias is not None:
                plain = width - bias.shape[1]
                biased = s[:, plain:] + bias
                s = biased if plain == 0 else jnp.concatenate([s[:, :plain], biased], axis=1)
            if online:
                m_prev = m_ref[mp]
                m_new = jnp.maximum(m_prev, jnp.max(s, axis=-1, keepdims=True))
                alpha = jnp.exp2(m_prev - m_new)
                p = jnp.exp2(s - m_new)
                l_ref[mp] = alpha * l_ref[mp] + lane_chunk_sum(p)
                acc_ref[mp] = alpha * acc_ref[mp] + _dot(p.astype(BF16), v)
                m_ref[mp] = m_new
            else:
                p = jnp.exp2(s)
                l_ref[mp] += lane_chunk_sum(p)
                acc_ref[mp] += _dot(p.astype(BF16), v)

    n_far = jnp.maximum(qb - 1, 0)
    n_groups = n_far // FAR_GROUP
    n_left = n_far - n_groups * FAR_GROUP

    def far_group(i, carry):
        block(pl.multiple_of(i * (FAR_GROUP * t), FAR_GROUP * t), FAR_GROUP * t, None)
        return carry

    lax.fori_loop(0, n_groups, far_group, 0)

    for left in range(FAR_GROUP):
        @pl.when(jnp.logical_and(qb > 0, n_left == left))
        def _(left=left):
            block(pl.multiple_of((qb - 1 - left) * t, t), (left + 2) * t, bias_ref[0])

    @pl.when(qb == 0)
    def _():
        block(0, t, bias_ref[0, :, t:2 * t])

    lv = lv_ref[...]
    lam = (jnp.exp(jnp.sum(lv[0:1] * lv[1:2], axis=-1, keepdims=True))
           - jnp.exp(jnp.sum(lv[2:3] * lv[3:4], axis=-1, keepdims=True)) + lambda_init)
    l1 = jnp.sum(l_ref[0], axis=-1, keepdims=True)
    l2 = jnp.sum(l_ref[1], axis=-1, keepdims=True)
    o = acc_ref[0] / l1 - lam * (acc_ref[1] / l2)
    o_ref[...] = (_rms(o, sg_ref[...]) * (1.0 - lambda_init)).astype(o_ref.dtype)


def _diff_attention(qk, v, bias, lam_vecs, subln_gain, lambda_init, logit_bound, *, t):
    s = qk.shape[0]
    t = min(t, s)
    assert s >= (FAR_GROUP + 1) * t
    hw = DA_V_DIM

    return pl.pallas_call(
        functools.partial(_attn_kernel, t=t, lambda_init=lambda_init),
        grid=(DA_HEADS, s // t),
        in_specs=[
            pl.BlockSpec(memory_space=pltpu.SMEM),
            pl.BlockSpec((t, hw), lambda h, i: (i, h)),
            pl.BlockSpec((s, hw), lambda h, i: (0, DA_HEADS + h)),
            pl.BlockSpec((s, hw), lambda h, i: (0, h)),
            pl.BlockSpec((1, t, 2 * t), lambda h, i: (h, 0, 0)),
            pl.BlockSpec((4, DA_HEAD_DIM), lambda h, i: (0, 0)),
            pl.BlockSpec((1, hw), lambda h, i: (0, 0)),
        ],
        out_specs=pl.BlockSpec((t, hw), lambda h, i: (i, h)),
        out_shape=jax.ShapeDtypeStruct((s, DA_HEADS * hw), BF16),
        scratch_shapes=[
            pltpu.VMEM((2, t, 1), F32),
            pltpu.VMEM((2, t, LANES), F32),
            pltpu.VMEM((2, t, hw), F32),
        ],
        compiler_params=_params("parallel", "arbitrary"),
        name="diff_attn",
    )(logit_bound.reshape(1), qk, qk, v, bias, lam_vecs, subln_gain.reshape(1, hw))


def _ssd_kernel(z0_ref, z1_ref, x0_ref, x1_ref, bc_ref, dt_ref, dtb_ref, alog_ref, dskip_ref,
                ng_ref, o_ref, bcs_ref, y_ref, state_ref):
    c = pl.program_id(0)
    ln = SSD_CHUNK
    cbw = SSD_COL_BLOCK
    ns = SSD_STATE

    @pl.when(c == 0)
    def _():
        state_ref[...] = jnp.zeros(state_ref.shape, F32)

    bcs_ref[...] = bc_ref[...].astype(BF16)
    x_refs = (x0_ref, x1_ref)

    dt = jax.nn.softplus(dt_ref[:, 0:SSD_HEADS] + dtb_ref[...])
    a = dt * (-LOG2E * jnp.exp(alog_ref[...]))
    row = lax.broadcasted_iota(jnp.int32, (ln, ln), 0)
    col = lax.broadcasted_iota(jnp.int32, (ln, ln), 1)
    causal = row >= col
    acum = jnp.dot(causal.astype(F32), a, preferred_element_type=F32, precision=lax.Precision.HIGHEST)
    acum_t = acum.T
    lane_lo = lax.broadcasted_iota(jnp.int32, (ln, LANES), 1) < SSD_HEAD_DIM

    for g in range(SSD_GROUPS):
        b_g = bcs_ref[:, g * ns:(g + 1) * ns]
        c_g = bcs_ref[:, SSD_GROUPS * ns + g * ns:SSD_GROUPS * ns + (g + 1) * ns]
        cb = _dot_nt(c_g, b_g)
        y_state = _dot(c_g, state_ref[g].astype(BF16))
        xw_parts = []
        decay_parts = []
        for pr in range(SSD_HEADS_PER_GROUP // 2):
            h0 = g * SSD_HEADS_PER_GROUP + 2 * pr
            ch = slice(h0 * SSD_HEAD_DIM, h0 * SSD_HEAD_DIM + LANES)
            m_parts = []
            a_cols = []
            for h in (h0, h0 + 1):
                a_col = jnp.broadcast_to(acum[:, h:h + 1], (ln, LANES))
                seg = a_col - acum_t[h:h + 1, :]
                m_parts.append((cb * jnp.where(causal, jnp.exp2(seg), 0.0)).astype(BF16))
                a_cols.append(a_col)
            a_pair = jnp.where(lane_lo, a_cols[0], a_cols[1])
            dt_pair = jnp.where(lane_lo, dt[:, h0:h0 + 1], dt[:, h0 + 1:h0 + 2])
            x_pair = x_refs[ch.start // cbw][:, ch.start % cbw:ch.start % cbw + LANES]
            xdt = x_pair * dt_pair
            xdt_b = xdt.astype(BF16)
            zero = jnp.zeros_like(xdt_b)
            rhs = jnp.concatenate([jnp.where(lane_lo, xdt_b, zero), jnp.where(lane_lo, zero, xdt_b)], axis=0)
            y = _dot(jnp.concatenate(m_parts, axis=1), rhs)
            y = y + y_state[:, pr * LANES:(pr + 1) * LANES] * jnp.exp2(a_pair)
            y_ref[:, ch] = y + x_pair * dskip_ref[:, ch]
            a_last = a_pair[ln - 1:ln, :]
            xw_parts.append((xdt * jnp.exp2(a_last - a_pair)).astype(BF16))
            decay_parts.append(jnp.exp2(a_last))
        xw = jnp.concatenate(xw_parts, axis=1)
        decay = jnp.concatenate(decay_parts, axis=1)
        state_ref[g] = state_ref[g] * decay + _dot_tn(b_g, xw)

    gated = jnp.concatenate([y_ref[:, 0:cbw] * z0_ref[...], y_ref[:, cbw:2 * cbw] * z1_ref[...]], axis=1)
    o_ref[...] = _rms(gated, ng_ref[...]).astype(o_ref.dtype)


def _ssd_scan(z_act, xbc, dt_raw, dt_bias, a_log, d_skip, norm_g):
    s = z_act.shape[0]
    ln = SSD_CHUNK
    cbw = SSD_COL_BLOCK
    assert SSD_INNER == 2 * cbw and 2 * SSD_GROUPS * SSD_STATE == cbw

    def col_block(j):
        return pl.BlockSpec((ln, cbw), lambda c: (c, j))

    def whole(shape):
        return pl.BlockSpec(shape, lambda c: (0,) * len(shape))

    return pl.pallas_call(
        _ssd_kernel,
        grid=(s // ln,),
        in_specs=[
            col_block(0), col_block(1), col_block(0), col_block(1), col_block(2),
            pl.BlockSpec((ln, LANES), lambda c: (c, 0)),
            whole((1, SSD_HEADS)), whole((1, SSD_HEADS)),
            whole((1, SSD_INNER)), whole((1, SSD_INNER)),
        ],
        out_specs=pl.BlockSpec((ln, SSD_INNER), lambda c: (c, 0)),
        out_shape=jax.ShapeDtypeStruct((s, SSD_INNER), BF16),
        scratch_shapes=[
            pltpu.VMEM((ln, 2 * SSD_GROUPS * SSD_STATE), BF16),
            pltpu.VMEM((ln, SSD_INNER), F32),
            pltpu.VMEM((SSD_GROUPS, SSD_STATE, SSD_GROUP_WIDTH), F32),
        ],
        compiler_params=_params("arbitrary"),
        name="ssd_scan",
    )(z_act, z_act, xbc, xbc, xbc, dt_raw,
      dt_bias.reshape(1, SSD_HEADS), a_log.reshape(1, SSD_HEADS),
      jnp.repeat(d_skip, SSD_HEAD_DIM).reshape(1, SSD_INNER), norm_g.reshape(1, SSD_INNER))


def _head_rms(y, gain, n_heads):
    return [_rms(y[:, h * XA_HEAD_DIM:(h + 1) * XA_HEAD_DIM], gain) for h in range(n_heads)]


def _mem_kv_kernel(mem_ref, g_ref, w_ref, kg_ref, k_ref, v_ref):
    xw = XA_HEADS * XA_HEAD_DIM
    kv = _dot(_rms(mem_ref[...], g_ref[...]).astype(BF16), w_ref[...].astype(BF16))
    k_ref[...] = jnp.concatenate(_head_rms(kv[:, 0:xw], kg_ref[...], XA_HEADS), axis=1).astype(BF16)
    v_ref[...] = kv[:, xw:2 * xw].astype(BF16)


def _mem_kv(mem, g, w_kv, layer, k_gain):
    m, d = mem.shape
    xw = XA_HEADS * XA_HEAD_DIM

    def whole(shape):
        return pl.BlockSpec(shape, lambda i: (0,) * len(shape))

    return pl.pallas_call(
        _mem_kv_kernel,
        grid=(1,),
        in_specs=[whole((m, d)), whole((1, d)), pl.BlockSpec((None, d, 2 * xw), lambda i: (layer, 0, 0)),
                  whole((1, XA_HEAD_DIM))],
        out_specs=(whole((m, xw)), whole((m, xw))),
        out_shape=(jax.ShapeDtypeStruct((m, xw), BF16), jax.ShapeDtypeStruct((m, xw), BF16)),
        compiler_params=_params("arbitrary"),
        name="mem_kv",
    )(mem, g.reshape(1, d), w_kv, k_gain.reshape(1, XA_HEAD_DIM))


def _xattn_kernel(x_ref, g_ref, wq_ref, qg_ref, k_ref, v_ref, wo_ref, o_ref, wq_b_ref, wo_b_ref):
    @pl.when(pl.program_id(0) == 0)
    def _():
        wq_b_ref[...] = wq_ref[...].astype(BF16)
        wo_b_ref[...] = wo_ref[...].astype(BF16)

    x = x_ref[...]
    q = _dot(_rms(x, g_ref[...]).astype(BF16), wq_b_ref[...])
    heads = []
    for h, qh in enumerate(_head_rms(q, qg_ref[...], XA_HEADS)):
        cols = slice(h * XA_HEAD_DIM, (h + 1) * XA_HEAD_DIM)
        s = _dot_nt((qh * XA_HEAD_DIM ** -0.5).astype(BF16), k_ref[:, cols])
        p = jnp.exp(s - jnp.max(s, axis=-1, keepdims=True))
        p = p / jnp.sum(p, axis=-1, keepdims=True)
        heads.append(_dot(p.astype(BF16), v_ref[:, cols]))
    o = jnp.concatenate(heads, axis=1).astype(BF16)
    o_ref[...] = x + _dot(o, wo_b_ref[...])


def _xattn(x, g, w_q, q_gain, k, v, w_o, layer, *, tm=512):
    s, d = x.shape
    m, xw = k.shape
    tm = min(tm, s)

    def whole(shape):
        return pl.BlockSpec(shape, lambda i: (0,) * len(shape))

    return pl.pallas_call(
        _xattn_kernel,
        grid=(s // tm,),
        in_specs=[
            pl.BlockSpec((tm, d), lambda i: (i, 0)),
            whole((1, d)), pl.BlockSpec((None, d, xw), lambda i: (layer, 0, 0)), whole((1, XA_HEAD_DIM)),
            whole((m, xw)), whole((m, xw)), pl.BlockSpec((None, xw, d), lambda i: (layer, 0, 0)),
        ],
        out_specs=pl.BlockSpec((tm, d), lambda i: (i, 0)),
        out_shape=jax.ShapeDtypeStruct((s, d), F32),
        scratch_shapes=[pltpu.VMEM((d, xw), BF16), pltpu.VMEM((xw, d), BF16)],
        compiler_params=_params("arbitrary"),
        name="mem_xattn",
    )(x, g.reshape(1, d), w_q, q_gain.reshape(1, XA_HEAD_DIM), k, v, w_o)


def _attn_layer(x, h, w_qkv, w_o, layer, q_gain, k_gain, lam_vecs, subln_gain, rel_bias, bias, lambda_init, t):
    d = x.shape[1]
    n_maps = 2 * DA_HEADS
    q_col_gain = q_gain * (DA_HEAD_DIM ** -0.5 * LOG2E)
    col_gain = jnp.concatenate([jnp.tile(q_col_gain, n_maps), jnp.tile(k_gain, n_maps)]).reshape(1, 2 * d)
    qk = _proj(h, w_qkv, layer, mode="qk_norm", n_cols=2 * d, extra=(col_gain,), out_dtype=BF16)
    v = _proj(h, w_qkv, layer, mode="plain", n_cols=d, first_col=2 * d, out_dtype=BF16)
    logit_bound = (DA_HEAD_DIM * jnp.max(jnp.abs(q_col_gain)) * jnp.max(jnp.abs(k_gain))
                   + LOG2E * jnp.max(jnp.abs(rel_bias - rel_bias[REL_BUCKETS - 1])))
    o = _diff_attention(qk, v, bias, lam_vecs, subln_gain, lambda_init, logit_bound, t=t)
    return _proj(o, w_o, layer, mode="residual", n_cols=d, extra=(x,), out_dtype=F32)


def _ssd_layer(x, h, w_in, w_out, layer, conv_w, conv_b, dt_bias, a_log, d_skip, ssd_norm):
    n_main = 2 * SSD_INNER + 2 * SSD_GROUPS * SSD_STATE
    w_in_t = jnp.swapaxes(w_in, 1, 2)
    z_act = _proj(h, w_in_t, layer, mode="silu", n_cols=SSD_INNER, out_dtype=F32, transposed=True)
    xbc = _proj(h, w_in_t, layer, mode="conv", n_cols=n_main - SSD_INNER, first_col=SSD_INNER,
                extra=(conv_w, conv_b.reshape(1, -1)), out_dtype=F32, transposed=True)
    dt_raw = _proj(h, w_in_t, layer, mode="plain", n_cols=LANES, first_col=n_main, out_dtype=F32, tn=LANES,
                   transposed=True)
    y = _ssd_scan(z_act, xbc, dt_raw, dt_bias, a_log, d_skip, ssd_norm)
    return _proj(y, w_out, layer, mode="residual", n_cols=x.shape[1], extra=(x,), out_dtype=F32, tn=512)


@jax.jit
def kernel(x, mem, rel_bias, ffn1_norm, ffn1_w_gate, ffn1_w_up, ffn1_w_down, mix_norm, attn_w_qkv, attn_w_o,
           attn_q_norm, attn_k_norm, attn_lambda, attn_subln, ssd_w_in, ssd_conv_w, ssd_conv_b, ssd_dt_bias,
           ssd_a_log, ssd_d, ssd_norm, ssd_w_out, xattn_norm, mem_norm, xattn_w_q, xattn_w_kv, xattn_w_o,
           xattn_q_norm, xattn_k_norm, ffn2_norm, ffn2_w_gate, ffn2_w_up, ffn2_w_down):
    b, s, d = x.shape
    assert b == 1
    t = min(ATTN_BLOCK, s)
    xs = x[0]
    mem2 = mem[0]
    bias = _bias_tiles(rel_bias, t)
    for i in range(DEPTH):
        xs, h_mix = _ffn(xs, ffn1_norm[i], ffn1_w_gate, ffn1_w_up, ffn1_w_down, i, next_gain=mix_norm[i])
        j = i // N_MIXERS
        if i % N_MIXERS == 0:
            lambda_init = 0.8 - 0.6 * math.exp(-0.3 * i)
            xs = _attn_layer(xs, h_mix, attn_w_qkv, attn_w_o, j, attn_q_norm[j], attn_k_norm[j],
                             attn_lambda[j], attn_subln[j], rel_bias, bias, lambda_init, t)
        else:
            xs = _ssd_layer(xs, h_mix, ssd_w_in, ssd_w_out, j, ssd_conv_w[j], ssd_conv_b[j], ssd_dt_bias[j],
                            ssd_a_log[j], ssd_d[j], ssd_norm[j])
        k, v = _mem_kv(mem2, mem_norm[i], xattn_w_kv, i, xattn_k_norm[i])
        xs = _xattn(xs, xattn_norm[i], xattn_w_q, xattn_q_norm[i], k, v, xattn_w_o, i)
        xs = _ffn(xs, ffn2_norm[i], ffn2_w_gate, ffn2_w_up, ffn2_w_down, i)
    return xs[None]
```

```python
import functools
import math

import jax
import jax.numpy as jnp
from jax import lax
from jax.experimental import pallas as pl
from jax.experimental.pallas import tpu as pltpu

F32 = jnp.float32
BF16 = jnp.bfloat16

EPS = 1e-6
NEG_INF = -1e30
LOG2E = math.log2(math.e)

DEPTH = 4
N_MIXERS = 2

DA_HEADS = 8
DA_HEAD_DIM = 128
DA_V_DIM = 2 * DA_HEAD_DIM
REL_BUCKETS = 32
REL_MAX_DIST = 128
ATTN_BLOCK = 512
FAR_GROUP = 4
MAX_UNSHIFTED_LOG2_LOGIT = 100.0

SSD_HEAD_DIM = 64
SSD_HEADS = 64
SSD_GROUPS = 8
SSD_STATE = 128
SSD_CONV = 4
SSD_CHUNK = 128
SSD_CHUNKS_PER_STEP = 2
SSD_INNER = SSD_HEADS * SSD_HEAD_DIM
SSD_HEADS_PER_GROUP = SSD_HEADS // SSD_GROUPS
SSD_GROUP_WIDTH = SSD_HEADS_PER_GROUP * SSD_HEAD_DIM
SSD_COL_BLOCK = 2048
CONV_HALO = 8
FFN_FINAL_SUB_ROWS = 128
PROJ_SUB_ROWS = 256

XA_HEADS = 4
XA_HEAD_DIM = 128

LANES = 128
VMEM_LIMIT_BYTES = 56 * 1024 * 1024


def _params(*sem):
    return pltpu.CompilerParams(dimension_semantics=sem, vmem_limit_bytes=VMEM_LIMIT_BYTES)


def _rms(x, g):
    return x * lax.rsqrt(jnp.mean(x * x, axis=-1, keepdims=True) + EPS) * g


def _silu(x):
    half = 0.5 * x
    return half + half * jnp.tanh(half)


def _dot(a, b):
    return jnp.dot(a, b, preferred_element_type=F32)


def _dot_nt(a, b):
    return lax.dot_general(a, b, (((1,), (1,)), ((), ())), preferred_element_type=F32)


def _dot_tn(a, b):
    return lax.dot_general(a, b, (((0,), (0,)), ((), ())), preferred_element_type=F32)


def _ffn_kernel(*refs, emit_next_norm):
    if emit_next_norm:
        x_ref, g_ref, wg_ref, wu_ref, wd_ref, gn_ref, o_ref, h_ref = refs
    else:
        x_ref, g_ref, wg_ref, wu_ref, wd_ref, o_ref, h_ref = refs
    j = pl.program_id(1)

    @pl.when(j == 0)
    def _():
        h_ref[...] = _rms(x_ref[...], g_ref[...]).astype(BF16)
        o_ref[...] = jnp.zeros_like(o_ref)

    h = h_ref[...]
    gate = _dot(h, wg_ref[...].astype(BF16))
    up = _dot(h, wu_ref[...].astype(BF16))
    a = (_silu(gate) * up).astype(BF16)
    o_ref[...] += _dot(a, wd_ref[...].astype(BF16))

    @pl.when(j == pl.num_programs(1) - 1)
    def _():
        if emit_next_norm:
            tm = x_ref.shape[0]
            sub = min(FFN_FINAL_SUB_ROWS, tm)
            for r in range(tm // sub):
                rows = slice(r * sub, (r + 1) * sub)
                out = x_ref[rows, :] + 0.5 * o_ref[rows, :]
                o_ref[rows, :] = out
                h_ref[rows, :] = _rms(out, gn_ref[...]).astype(BF16)
        else:
            o_ref[...] = x_ref[...] + 0.5 * o_ref[...]


def _ffn(x, g, wg, wu, wd, layer, next_gain=None, *, tm=1024, tf=256):
    s, d = x.shape
    f = wg.shape[2]
    tm = min(tm, s)
    emit = next_gain is not None
    row_tile = pl.BlockSpec((tm, d), lambda i, j: (i, 0))
    gain = pl.BlockSpec((1, d), lambda i, j: (0, 0))
    in_specs = [
        row_tile, gain,
        pl.BlockSpec((None, d, tf), lambda i, j: (layer, 0, j)),
        pl.BlockSpec((None, d, tf), lambda i, j: (layer, 0, j)),
        pl.BlockSpec((None, tf, d), lambda i, j: (layer, j, 0)),
    ]
    operands = [x, g.reshape(1, d), wg, wu, wd]
    if emit:
        in_specs.append(gain)
        operands.append(next_gain.reshape(1, d))
    return pl.pallas_call(
        functools.partial(_ffn_kernel, emit_next_norm=emit),
        grid=(s // tm, f // tf),
        in_specs=in_specs,
        out_specs=(row_tile, row_tile) if emit else row_tile,
        out_shape=((jax.ShapeDtypeStruct((s, d), F32), jax.ShapeDtypeStruct((s, d), BF16)) if emit
                   else jax.ShapeDtypeStruct((s, d), F32)),
        scratch_shapes=[] if emit else [pltpu.VMEM((tm, d), BF16)],
        compiler_params=_params("parallel", "arbitrary"),
        name="ffn",
    )(*operands)


def _proj_kernel(*refs, mode, n_valid, transposed):
    if mode == "qk_norm":
        h_ref, w_ref, cg_ref, o_ref, wb_ref = refs
    elif mode == "residual":
        h_ref, w_ref, x_ref, o_ref, wb_ref = refs
    elif mode == "conv":
        h_ref, w_ref, cw_ref, cb_ref, o_ref, wb_ref, halo_ref = refs
    else:
        h_ref, w_ref, o_ref, wb_ref = refs
    j = pl.program_id(0)
    i = pl.program_id(1)

    @pl.when(i == 0)
    def _():
        w = w_ref[...]
        col_axis = 0 if transposed else 1
        if n_valid < w.shape[col_axis]:
            limit = jnp.where(j == pl.num_programs(0) - 1, n_valid, w.shape[col_axis])
            w = jnp.where(lax.broadcasted_iota(jnp.int32, w.shape, col_axis) < limit, w, 0.0)
        wb_ref[...] = w.astype(BF16)
        if mode == "conv":
            halo_ref[...] = jnp.zeros(halo_ref.shape, F32)

    tm = h_ref.shape[0]
    sub = min(PROJ_SUB_ROWS, tm)
    halo = halo_ref[...] if mode == "conv" else None
    for r in range(tm // sub):
        rows = slice(r * sub, (r + 1) * sub)
        y = (_dot_nt if transposed else _dot)(h_ref[rows, :], wb_ref[...])
        if mode == "qk_norm":
            for c in range(y.shape[1] // LANES):
                cols = slice(c * LANES, (c + 1) * LANES)
                o_ref[rows, cols] = _rms(y[:, cols], cg_ref[:, cols]).astype(o_ref.dtype)
        elif mode == "silu":
            o_ref[rows, :] = _silu(y).astype(o_ref.dtype)
        elif mode == "residual":
            o_ref[rows, :] = x_ref[rows, :] + y
        elif mode == "conv":
            ext = jnp.concatenate([halo, y], axis=0)
            acc = cb_ref[...] + cw_ref[SSD_CONV - 1:SSD_CONV, :] * y
            for k in range(SSD_CONV - 1):
                shifted = pltpu.roll(ext, SSD_CONV - 1 - k, axis=0)[CONV_HALO:, :]
                acc = acc + cw_ref[k:k + 1, :] * shifted
            halo = y[sub - CONV_HALO:, :]
            o_ref[rows, :] = _silu(acc).astype(o_ref.dtype)
        else:
            o_ref[rows, :] = y.astype(o_ref.dtype)
    if mode == "conv":
        halo_ref[...] = halo


def _proj(h, w, layer, *, mode, n_cols, first_col=0, extra=(), out_dtype, transposed=False, tm=1024, tn=1024):
    s, d = h.shape
    tm = min(tm, s)
    tn = min(tn, n_cols)
    assert n_cols % tn == 0 and first_col % tn == 0
    col0 = first_col // tn
    w_cols = w.shape[1] if transposed else w.shape[2]
    n_valid = min(tn, w_cols - first_col - (n_cols - tn))
    if transposed:
        w_spec = pl.BlockSpec((None, tn, d), lambda j, i: (layer, col0 + j, 0))
    else:
        w_spec = pl.BlockSpec((None, d, tn), lambda j, i: (layer, 0, col0 + j))
    in_specs = [pl.BlockSpec((tm, d), lambda j, i: (i, 0)), w_spec]
    if mode == "residual":
        in_specs.append(pl.BlockSpec((tm, tn), lambda j, i: (i, j)))
    else:
        in_specs += [pl.BlockSpec((e.shape[0], tn), lambda j, i: (0, j)) for e in extra]
    scratch = [pltpu.VMEM((tn, d) if transposed else (d, tn), BF16)]
    if mode == "conv":
        scratch.append(pltpu.VMEM((CONV_HALO, tn), F32))
    return pl.pallas_call(
        functools.partial(_proj_kernel, mode=mode, n_valid=n_valid, transposed=transposed),
        grid=(n_cols // tn, s // tm),
        in_specs=in_specs,
        out_specs=pl.BlockSpec((tm, tn), lambda j, i: (i, j)),
        out_shape=jax.ShapeDtypeStruct((s, n_cols), out_dtype),
        scratch_shapes=scratch,
        compiler_params=_params("arbitrary", "arbitrary"),
        name="proj_" + mode,
    )(h, w, *extra)


def _bias_kernel(tab_ref, o_ref, *, t):
    h = pl.program_id(0)
    sb = LANES
    row = lax.broadcasted_iota(jnp.int32, (sb, sb), 0)
    col = lax.broadcasted_iota(jnp.int32, (sb, sb), 1)
    max_exact = REL_BUCKETS // 2
    far = tab_ref[REL_BUCKETS - 1, h]

    def sub_block(offset):
        dist = row - col + offset
        n = jnp.maximum(dist, 0)
        nf = jnp.maximum(n, 1).astype(F32)
        large = max_exact + (jnp.log(nf / max_exact) / math.log(REL_MAX_DIST / max_exact)
                             * (REL_BUCKETS - max_exact)).astype(jnp.int32)
        large = jnp.minimum(large, REL_BUCKETS - 1)
        bucket = jnp.where(n < max_exact, n, large)
        bias = jnp.zeros((sb, sb), F32)
        for b in range(REL_BUCKETS - 1):
            bias = jnp.where(bucket == b, (tab_ref[b, h] - far) * LOG2E, bias)
        return jnp.where(dist >= 0, bias, NEG_INF)

    for blk in range(2):
        for i in range(t // sb):
            for j in range(t // sb):
                offset = blk * t + (i - j) * sb
                if offset - (sb - 1) >= REL_MAX_DIST:
                    tile = jnp.zeros((sb, sb), F32)
                elif offset + (sb - 1) < 0:
                    tile = jnp.full((sb, sb), NEG_INF, F32)
                else:
                    tile = sub_block(offset)
                o_ref[0, i * sb:(i + 1) * sb, (1 - blk) * t + j * sb:(1 - blk) * t + (j + 1) * sb] = tile


def _bias_tiles(rel_bias, t):
    assert t >= REL_MAX_DIST
    return pl.pallas_call(
        functools.partial(_bias_kernel, t=t),
        grid=(DA_HEADS,),
        in_specs=[pl.BlockSpec(memory_space=pltpu.SMEM)],
        out_specs=pl.BlockSpec((1, t, 2 * t), lambda h: (h, 0, 0)),
        out_shape=jax.ShapeDtypeStruct((DA_HEADS, t, 2 * t), F32),
        compiler_params=_params("arbitrary"),
        name="rel_bias_tiles",
    )(rel_bias)


def _attn_kernel(q_ref, k_ref, v_ref, bias_ref, lv_ref, sg_ref, o_ref, m_ref, l_ref, acc_ref, *, t, lambda_init,
                 online):
    qb = pl.program_id(1)
    dh = DA_HEAD_DIM
    if online:
        m_ref[...] = jnp.full(m_ref.shape, NEG_INF, F32)
    l_ref[...] = jnp.zeros(l_ref.shape, F32)
    acc_ref[...] = jnp.zeros(acc_ref.shape, F32)

    def lane_chunk_sum(p):
        out = p[:, 0:LANES]
        for c in range(1, p.shape[1] // LANES):
            out = out + p[:, c * LANES:(c + 1) * LANES]
        return out

    def block(start, width, bias):
        v = v_ref[pl.ds(start, width), :]
        for mp in range(2):
            cols = slice(mp * dh, (mp + 1) * dh)
            s = _dot_nt(q_ref[:, cols], k_ref[pl.ds(start, width), cols])
            if bias is not None:
                plain = width - bias.shape[1]
                biased = s[:, plain:] + bias
                s = biased if plain == 0 else jnp.concatenate([s[:, :plain], biased], axis=1)
            if online:
                m_prev = m_ref[mp]
                m_new = jnp.maximum(m_prev, jnp.max(s, axis=-1, keepdims=True))
                alpha = jnp.exp2(m_prev - m_new)
                p = jnp.exp2(s - m_new)
                l_ref[mp] = alpha * l_ref[mp] + lane_chunk_sum(p)
                acc_ref[mp] = alpha * acc_ref[mp] + _dot(p.astype(BF16), v)
                m_ref[mp] = m_new
            else:
                p = jnp.exp2(s)
                l_ref[mp] += lane_chunk_sum(p)
                acc_ref[mp] += _dot(p.astype(BF16), v)

    n_far = jnp.maximum(qb - 1, 0)
    n_groups = n_far // FAR_GROUP
    n_left = n_far - n_groups * FAR_GROUP

    def far_group(i, carry):
        block(pl.multiple_of(i * (FAR_GROUP * t), FAR_GROUP * t), FAR_GROUP * t, None)
        return carry

    lax.fori_loop(0, n_groups, far_group, 0)

    for left in range(FAR_GROUP):
        @pl.when(jnp.logical_and(qb > 0, n_left == left))
        def _(left=left):
            block(pl.multiple_of((qb - 1 - left) * t, t), (left + 2) * t, bias_ref[0])

    @pl.when(qb == 0)
    def _():
        block(0, t, bias_ref[0, :, t:2 * t])

    lv = lv_ref[...]
    lam = (jnp.exp(jnp.sum(lv[0:1] * lv[1:2], axis=-1, keepdims=True))
           - jnp.exp(jnp.sum(lv[2:3] * lv[3:4], axis=-1, keepdims=True)) + lambda_init)
    l1 = jnp.sum(l_ref[0], axis=-1, keepdims=True)
    l2 = jnp.sum(l_ref[1], axis=-1, keepdims=True)
    o = acc_ref[0] / l1 - lam * (acc_ref[1] / l2)
    o_ref[...] = (_rms(o, sg_ref[...]) * (1.0 - lambda_init)).astype(o_ref.dtype)


def _diff_attention(qk, v, bias, lam_vecs, subln_gain, lambda_init, logit_bound, *, t):
    s = qk.shape[0]
    t = min(t, s)
    assert s >= (FAR_GROUP + 1) * t
    hw = DA_V_DIM

    def call(online):
        return pl.pallas_call(
            functools.partial(_attn_kernel, t=t, lambda_init=lambda_init, online=online),
            grid=(DA_HEADS, s // t),
            in_specs=[
                pl.BlockSpec((t, hw), lambda h, i: (i, h)),
                pl.BlockSpec((s, hw), lambda h, i: (0, DA_HEADS + h)),
                pl.BlockSpec((s, hw), lambda h, i: (0, h)),
                pl.BlockSpec((1, t, 2 * t), lambda h, i: (h, 0, 0)),
                pl.BlockSpec((4, DA_HEAD_DIM), lambda h, i: (0, 0)),
                pl.BlockSpec((1, hw), lambda h, i: (0, 0)),
            ],
            out_specs=pl.BlockSpec((t, hw), lambda h, i: (i, h)),
            out_shape=jax.ShapeDtypeStruct((s, DA_HEADS * hw), BF16),
            scratch_shapes=[
                pltpu.VMEM((2, t, 1), F32),
                pltpu.VMEM((2, t, LANES), F32),
                pltpu.VMEM((2, t, hw), F32),
            ],
            compiler_params=_params("parallel", "arbitrary"),
            name="diff_attn_online" if online else "diff_attn",
        )

    operands = (qk, qk, v, bias, lam_vecs, subln_gain.reshape(1, hw))
    return lax.cond(logit_bound < MAX_UNSHIFTED_LOG2_LOGIT,
                    lambda *a: call(False)(*a), lambda *a: call(True)(*a), *operands)


def _ssd_kernel(z0_ref, z1_ref, x0_ref, x1_ref, bc_ref, dt_ref, dtb_ref, alog_ref, dskip_ref,
                ng_ref, o_ref, bcs_ref, y_ref, state_ref):
    ln = SSD_CHUNK
    cbw = SSD_COL_BLOCK
    ns = SSD_STATE

    @pl.when(pl.program_id(0) == 0)
    def _():
        state_ref[...] = jnp.zeros(state_ref.shape, F32)

    x_refs = (x0_ref, x1_ref)
    neg_a = -LOG2E * jnp.exp(alog_ref[...])
    row = lax.broadcasted_iota(jnp.int32, (ln, ln), 0)
    col = lax.broadcasted_iota(jnp.int32, (ln, ln), 1)
    causal = row >= col
    lane_lo = lax.broadcasted_iota(jnp.int32, (ln, LANES), 1) < SSD_HEAD_DIM

    for ck in range(z0_ref.shape[0] // ln):
        rows = slice(ck * ln, (ck + 1) * ln)
        bcs_ref[...] = bc_ref[rows, :].astype(BF16)
        dt = jax.nn.softplus(dt_ref[rows, 0:SSD_HEADS] + dtb_ref[...])
        a = dt * neg_a
        acum = jnp.dot(causal.astype(F32), a, preferred_element_type=F32, precision=lax.Precision.HIGHEST)
        acum_t = acum.T

        for g in range(SSD_GROUPS):
            b_g = bcs_ref[:, g * ns:(g + 1) * ns]
            c_g = bcs_ref[:, SSD_GROUPS * ns + g * ns:SSD_GROUPS * ns + (g + 1) * ns]
            cb = _dot_nt(c_g, b_g)
            y_state = _dot(c_g, state_ref[g].astype(BF16))
            xw_parts = []
            decay_parts = []
            for pr in range(SSD_HEADS_PER_GROUP // 2):
                h0 = g * SSD_HEADS_PER_GROUP + 2 * pr
                ch = slice(h0 * SSD_HEAD_DIM, h0 * SSD_HEAD_DIM + LANES)
                m_parts = []
                a_cols = []
                for h in (h0, h0 + 1):
                    a_col = jnp.broadcast_to(acum[:, h:h + 1], (ln, LANES))
                    seg = a_col - acum_t[h:h + 1, :]
                    m_parts.append((cb * jnp.where(causal, jnp.exp2(seg), 0.0)).astype(BF16))
                    a_cols.append(a_col)
                a_pair = jnp.where(lane_lo, a_cols[0], a_cols[1])
                dt_pair = jnp.where(lane_lo, dt[:, h0:h0 + 1], dt[:, h0 + 1:h0 + 2])
                x_pair = x_refs[ch.start // cbw][rows, ch.start % cbw:ch.start % cbw + LANES]
                xdt = x_pair * dt_pair
                xdt_b = xdt.astype(BF16)
                zero = jnp.zeros_like(xdt_b)
                rhs = jnp.concatenate([jnp.where(lane_lo, xdt_b, zero), jnp.where(lane_lo, zero, xdt_b)], axis=0)
                y = _dot(jnp.concatenate(m_parts, axis=1), rhs)
                y = y + y_state[:, pr * LANES:(pr + 1) * LANES] * jnp.exp2(a_pair)
                y_ref[:, ch] = y + x_pair * dskip_ref[:, ch]
                a_last = a_pair[ln - 1:ln, :]
                xw_parts.append((xdt * jnp.exp2(a_last - a_pair)).astype(BF16))
                decay_parts.append(jnp.exp2(a_last))
            xw = jnp.concatenate(xw_parts, axis=1)
            decay = jnp.concatenate(decay_parts, axis=1)
            state_ref[g] = state_ref[g] * decay + _dot_tn(b_g, xw)

        gated = jnp.concatenate([y_ref[:, 0:cbw] * z0_ref[rows, :], y_ref[:, cbw:2 * cbw] * z1_ref[rows, :]], axis=1)
        o_ref[rows, :] = _rms(gated, ng_ref[...]).astype(o_ref.dtype)


def _ssd_scan(z_act, xbc, dt_raw, dt_bias, a_log, d_skip, norm_g):
    s = z_act.shape[0]
    ln = SSD_CHUNK
    rows = min(SSD_CHUNKS_PER_STEP * ln, s)
    cbw = SSD_COL_BLOCK
    assert SSD_INNER == 2 * cbw and 2 * SSD_GROUPS * SSD_STATE == cbw

    def col_block(j):
        return pl.BlockSpec((rows, cbw), lambda c: (c, j))

    def whole(shape):
        return pl.BlockSpec(shape, lambda c: (0,) * len(shape))

    return pl.pallas_call(
        _ssd_kernel,
        grid=(s // rows,),
        in_specs=[
            col_block(0), col_block(1), col_block(0), col_block(1), col_block(2),
            pl.BlockSpec((rows, LANES), lambda c: (c, 0)),
            whole((1, SSD_HEADS)), whole((1, SSD_HEADS)),
            whole((1, SSD_INNER)), whole((1, SSD_INNER)),
        ],
        out_specs=pl.BlockSpec((rows, SSD_INNER), lambda c: (c, 0)),
        out_shape=jax.ShapeDtypeStruct((s, SSD_INNER), BF16),
        scratch_shapes=[
            pltpu.VMEM((ln, 2 * SSD_GROUPS * SSD_STATE), BF16),
            pltpu.VMEM((ln, SSD_INNER), F32),
            pltpu.VMEM((SSD_GROUPS, SSD_STATE, SSD_GROUP_WIDTH), F32),
        ],
        compiler_params=_params("arbitrary"),
        name="ssd_scan",
    )(z_act, z_act, xbc, xbc, xbc, dt_raw,
      dt_bias.reshape(1, SSD_HEADS), a_log.reshape(1, SSD_HEADS),
      jnp.repeat(d_skip, SSD_HEAD_DIM).reshape(1, SSD_INNER), norm_g.reshape(1, SSD_INNER))


def _head_rms(y, gain, n_heads):
    return [_rms(y[:, h * XA_HEAD_DIM:(h + 1) * XA_HEAD_DIM], gain) for h in range(n_heads)]


def _mem_kv_kernel(mem_ref, g_ref, w_ref, kg_ref, k_ref, v_ref):
    xw = XA_HEADS * XA_HEAD_DIM
    kv = _dot(_rms(mem_ref[...], g_ref[...]).astype(BF16), w_ref[...].astype(BF16))
    k_ref[...] = jnp.concatenate(_head_rms(kv[:, 0:xw], kg_ref[...], XA_HEADS), axis=1).astype(BF16)
    v_ref[...] = kv[:, xw:2 * xw].astype(BF16)


def _mem_kv(mem, g, w_kv, layer, k_gain):
    m, d = mem.shape
    xw = XA_HEADS * XA_HEAD_DIM

    def whole(shape):
        return pl.BlockSpec(shape, lambda i: (0,) * len(shape))

    return pl.pallas_call(
        _mem_kv_kernel,
        grid=(1,),
        in_specs=[whole((m, d)), whole((1, d)), pl.BlockSpec((None, d, 2 * xw), lambda i: (layer, 0, 0)),
                  whole((1, XA_HEAD_DIM))],
        out_specs=(whole((m, xw)), whole((m, xw))),
        out_shape=(jax.ShapeDtypeStruct((m, xw), BF16), jax.ShapeDtypeStruct((m, xw), BF16)),
        compiler_params=_params("arbitrary"),
        name="mem_kv",
    )(mem, g.reshape(1, d), w_kv, k_gain.reshape(1, XA_HEAD_DIM))


def _xattn_kernel(x_ref, g_ref, wq_ref, qg_ref, k_ref, v_ref, wo_ref, o_ref, wq_b_ref, wo_b_ref):
    @pl.when(pl.program_id(0) == 0)
    def _():
        wq_b_ref[...] = wq_ref[...].astype(BF16)
        wo_b_ref[...] = wo_ref[...].astype(BF16)

    x = x_ref[...]
    q = _dot(_rms(x, g_ref[...]).astype(BF16), wq_b_ref[...])
    heads = []
    for h, qh in enumerate(_head_rms(q, qg_ref[...], XA_HEADS)):
        cols = slice(h * XA_HEAD_DIM, (h + 1) * XA_HEAD_DIM)
        s = _dot_nt((qh * XA_HEAD_DIM ** -0.5).astype(BF16), k_ref[:, cols])
        p = jnp.exp(s - jnp.max(s, axis=-1, keepdims=True))
        p = p / jnp.sum(p, axis=-1, keepdims=True)
        heads.append(_dot(p.astype(BF16), v_ref[:, cols]))
    o = jnp.concatenate(heads, axis=1).astype(BF16)
    o_ref[...] = x + _dot(o, wo_b_ref[...])


def _xattn(x, g, w_q, q_gain, k, v, w_o, layer, *, tm=512):
    s, d = x.shape
    m, xw = k.shape
    tm = min(tm, s)

    def whole(shape):
        return pl.BlockSpec(shape, lambda i: (0,) * len(shape))

    return pl.pallas_call(
        _xattn_kernel,
        grid=(s // tm,),
        in_specs=[
            pl.BlockSpec((tm, d), lambda i: (i, 0)),
            whole((1, d)), pl.BlockSpec((None, d, xw), lambda i: (layer, 0, 0)), whole((1, XA_HEAD_DIM)),
            whole((m, xw)), whole((m, xw)), pl.BlockSpec((None, xw, d), lambda i: (layer, 0, 0)),
        ],
        out_specs=pl.BlockSpec((tm, d), lambda i: (i, 0)),
        out_shape=jax.ShapeDtypeStruct((s, d), F32),
        scratch_shapes=[pltpu.VMEM((d, xw), BF16), pltpu.VMEM((xw, d), BF16)],
        compiler_params=_params("arbitrary"),
        name="mem_xattn",
    )(x, g.reshape(1, d), w_q, q_gain.reshape(1, XA_HEAD_DIM), k, v, w_o)


def _attn_layer(x, h, w_qkv, w_o, layer, q_gain, k_gain, lam_vecs, subln_gain, rel_bias, bias, lambda_init, t):
    d = x.shape[1]
    n_maps = 2 * DA_HEADS
    q_col_gain = q_gain * (DA_HEAD_DIM ** -0.5 * LOG2E)
    col_gain = jnp.concatenate([jnp.tile(q_col_gain, n_maps), jnp.tile(k_gain, n_maps)]).reshape(1, 2 * d)
    qk = _proj(h, w_qkv, layer, mode="qk_norm", n_cols=2 * d, extra=(col_gain,), out_dtype=BF16)
    v = _proj(h, w_qkv, layer, mode="plain", n_cols=d, first_col=2 * d, out_dtype=BF16)
    logit_bound = (DA_HEAD_DIM * jnp.max(jnp.abs(q_col_gain)) * jnp.max(jnp.abs(k_gain))
                   + LOG2E * jnp.max(jnp.abs(rel_bias - rel_bias[REL_BUCKETS - 1])))
    o = _diff_attention(qk, v, bias, lam_vecs, subln_gain, lambda_init, logit_bound, t=t)
    return _proj(o, w_o, layer, mode="residual", n_cols=d, extra=(x,), out_dtype=F32)


def _ssd_layer(x, h, w_in, w_out, layer, conv_w, conv_b, dt_bias, a_log, d_skip, ssd_norm):
    n_main = 2 * SSD_INNER + 2 * SSD_GROUPS * SSD_STATE
    w_in_t = jnp.swapaxes(w_in, 1, 2)
    z_act = _proj(h, w_in_t, layer, mode="silu", n_cols=SSD_INNER, out_dtype=F32, transposed=True)
    xbc = _proj(h, w_in_t, layer, mode="conv", n_cols=n_main - SSD_INNER, first_col=SSD_INNER,
                extra=(conv_w, conv_b.reshape(1, -1)), out_dtype=F32, transposed=True)
    dt_raw = _proj(h, w_in_t, layer, mode="plain", n_cols=LANES, first_col=n_main, out_dtype=F32, tn=LANES,
                   transposed=True)
    y = _ssd_scan(z_act, xbc, dt_raw, dt_bias, a_log, d_skip, ssd_norm)
    return _proj(y, w_out, layer, mode="residual", n_cols=x.shape[1], extra=(x,), out_dtype=F32, tn=512)


@jax.jit
def kernel(x, mem, rel_bias, ffn1_norm, ffn1_w_gate, ffn1_w_up, ffn1_w_down, mix_norm, attn_w_qkv, attn_w_o,
           attn_q_norm, attn_k_norm, attn_lambda, attn_subln, ssd_w_in, ssd_conv_w, ssd_conv_b, ssd_dt_bias,
           ssd_a_log, ssd_d, ssd_norm, ssd_w_out, xattn_norm, mem_norm, xattn_w_q, xattn_w_kv, xattn_w_o,
           xattn_q_norm, xattn_k_norm, ffn2_norm, ffn2_w_gate, ffn2_w_up, ffn2_w_down):
    b, s, d = x.shape
    assert b == 1
    t = min(ATTN_BLOCK, s)
    xs = x[0]
    mem2 = mem[0]
    bias = _bias_tiles(rel_bias, t)
    for i in range(DEPTH):
        xs, h_mix = _ffn(xs, ffn1_norm[i], ffn1_w_gate, ffn1_w_up, ffn1_w_down, i, next_gain=mix_norm[i])
        j = i // N_MIXERS
        if i % N_MIXERS == 0:
            lambda_init = 0.8 - 0.6 * math.exp(-0.3 * i)
            xs = _attn_layer(xs, h_mix, attn_w_qkv, attn_w_o, j, attn_q_norm[j], attn_k_norm[j],
                             attn_lambda[j], attn_subln[j], rel_bias, bias, lambda_init, t)
        else:
            xs = _ssd_layer(xs, h_mix, ssd_w_in, ssd_w_out, j, ssd_conv_w[j], ssd_conv_b[j], ssd_dt_bias[j],
                            ssd_a_log[j], ssd_d[j], ssd_norm[j])
        k, v = _mem_kv(mem2, mem_norm[i], xattn_w_kv, i, xattn_k_norm[i])
        xs = _xattn(xs, xattn_norm[i], xattn_w_q, xattn_q_norm[i], k, v, xattn_w_o, i)
        xs = _ffn(xs, ffn2_norm[i], ffn2_w_gate, ffn2_w_up, ffn2_w_down, i)
    return xs[None]
```

```python
import functools
import math

import jax
import jax.numpy as jnp
from jax import lax
from jax.experimental import pallas as pl
from jax.experimental.pallas import tpu as pltpu

F32 = jnp.float32
BF16 = jnp.bfloat16

EPS = 1e-6
NEG_INF = -1e30
LOG2E = math.log2(math.e)

DEPTH = 4
N_MIXERS = 2

DA_HEADS = 8
DA_HEAD_DIM = 128
DA_V_DIM = 2 * DA_HEAD_DIM
REL_BUCKETS = 32
REL_MAX_DIST = 128
ATTN_BLOCK = 512
FAR_GROUP = 4
MAX_UNSHIFTED_LOG2_LOGIT = 100.0

SSD_HEAD_DIM = 64
SSD_HEADS = 64
SSD_GROUPS = 8
SSD_STATE = 128
SSD_CONV = 4
SSD_CHUNK = 128
SSD_CHUNKS_PER_STEP = 1
SSD_INNER = SSD_HEADS * SSD_HEAD_DIM
SSD_HEADS_PER_GROUP = SSD_HEADS // SSD_GROUPS
SSD_GROUP_WIDTH = SSD_HEADS_PER_GROUP * SSD_HEAD_DIM
SSD_COL_BLOCK = 2048
CONV_HALO = 8
FFN_FINAL_SUB_ROWS = 128
PROJ_SUB_ROWS = 256

XA_HEADS = 4
XA_HEAD_DIM = 128

LANES = 128
VMEM_LIMIT_BYTES = 56 * 1024 * 1024


def _params(*sem):
    return pltpu.CompilerParams(dimension_semantics=sem, vmem_limit_bytes=VMEM_LIMIT_BYTES)


def _rms(x, g):
    return x * lax.rsqrt(jnp.mean(x * x, axis=-1, keepdims=True) + EPS) * g


def _silu(x):
    half = 0.5 * x
    return half + half * jnp.tanh(half)


def _dot(a, b):
    return jnp.dot(a, b, preferred_element_type=F32)


def _dot_nt(a, b):
    return lax.dot_general(a, b, (((1,), (1,)), ((), ())), preferred_element_type=F32)


def _dot_tn(a, b):
    return lax.dot_general(a, b, (((0,), (0,)), ((), ())), preferred_element_type=F32)


def _ffn_kernel(*refs, emit_next_norm):
    if emit_next_norm:
        x_ref, g_ref, wg_ref, wu_ref, wd_ref, gn_ref, o_ref, h_ref = refs
    else:
        x_ref, g_ref, wg_ref, wu_ref, wd_ref, o_ref, h_ref = refs
    j = pl.program_id(1)

    @pl.when(j == 0)
    def _():
        h_ref[...] = _rms(x_ref[...], g_ref[...]).astype(BF16)
        o_ref[...] = jnp.zeros_like(o_ref)

    h = h_ref[...]
    gate = _dot(h, wg_ref[...].astype(BF16))
    up = _dot(h, wu_ref[...].astype(BF16))
    a = (_silu(gate) * up).astype(BF16)
    o_ref[...] += _dot(a, wd_ref[...].astype(BF16))

    @pl.when(j == pl.num_programs(1) - 1)
    def _():
        if emit_next_norm:
            tm = x_ref.shape[0]
            sub = min(FFN_FINAL_SUB_ROWS, tm)
            for r in range(tm // sub):
                rows = slice(r * sub, (r + 1) * sub)
                out = x_ref[rows, :] + 0.5 * o_ref[rows, :]
                o_ref[rows, :] = out
                h_ref[rows, :] = _rms(out, gn_ref[...]).astype(BF16)
        else:
            o_ref[...] = x_ref[...] + 0.5 * o_ref[...]


def _ffn(x, g, wg, wu, wd, layer, next_gain=None, *, tm=1024, tf=256):
    s, d = x.shape
    f = wg.shape[2]
    tm = min(tm, s)
    emit = next_gain is not None
    row_tile = pl.BlockSpec((tm, d), lambda i, j: (i, 0))
    gain = pl.BlockSpec((1, d), lambda i, j: (0, 0))
    in_specs = [
        row_tile, gain,
        pl.BlockSpec((None, d, tf), lambda i, j: (layer, 0, j)),
        pl.BlockSpec((None, d, tf), lambda i, j: (layer, 0, j)),
        pl.BlockSpec((None, tf, d), lambda i, j: (layer, j, 0)),
    ]
    operands = [x, g.reshape(1, d), wg, wu, wd]
    if emit:
        in_specs.append(gain)
        operands.append(next_gain.reshape(1, d))
    return pl.pallas_call(
        functools.partial(_ffn_kernel, emit_next_norm=emit),
        grid=(s // tm, f // tf),
        in_specs=in_specs,
        out_specs=(row_tile, row_tile) if emit else row_tile,
        out_shape=((jax.ShapeDtypeStruct((s, d), F32), jax.ShapeDtypeStruct((s, d), BF16)) if emit
                   else jax.ShapeDtypeStruct((s, d), F32)),
        scratch_shapes=[] if emit else [pltpu.VMEM((tm, d), BF16)],
        compiler_params=_params("parallel", "arbitrary"),
        name="ffn",
    )(*operands)


def _proj_kernel(*refs, mode, n_valid, transposed):
    if mode == "qk_norm":
        h_ref, w_ref, cg_ref, o_ref, wb_ref = refs
    elif mode == "residual":
        h_ref, w_ref, x_ref, o_ref, wb_ref = refs
    elif mode == "conv":
        h_ref, w_ref, cw_ref, cb_ref, o_ref, wb_ref, halo_ref = refs
    else:
        h_ref, w_ref, o_ref, wb_ref = refs
    j = pl.program_id(0)
    i = pl.program_id(1)

    @pl.when(i == 0)
    def _():
        w = w_ref[...]
        col_axis = 0 if transposed else 1
        if n_valid < w.shape[col_axis]:
            limit = jnp.where(j == pl.num_programs(0) - 1, n_valid, w.shape[col_axis])
            w = jnp.where(lax.broadcasted_iota(jnp.int32, w.shape, col_axis) < limit, w, 0.0)
        wb_ref[...] = w.astype(BF16)
        if mode == "conv":
            halo_ref[...] = jnp.zeros(halo_ref.shape, F32)

    tm = h_ref.shape[0]
    sub = min(PROJ_SUB_ROWS, tm)
    halo = halo_ref[...] if mode == "conv" else None
    for r in range(tm // sub):
        rows = slice(r * sub, (r + 1) * sub)
        y = (_dot_nt if transposed else _dot)(h_ref[rows, :], wb_ref[...])
        if mode == "qk_norm":
            for c in range(y.shape[1] // LANES):
                cols = slice(c * LANES, (c + 1) * LANES)
                o_ref[rows, cols] = _rms(y[:, cols], cg_ref[:, cols]).astype(o_ref.dtype)
        elif mode == "silu":
            o_ref[rows, :] = _silu(y).astype(o_ref.dtype)
        elif mode == "residual":
            o_ref[rows, :] = x_ref[rows, :] + y
        elif mode == "conv":
            ext = jnp.concatenate([halo, y], axis=0)
            acc = cb_ref[...] + cw_ref[SSD_CONV - 1:SSD_CONV, :] * y
            for k in range(SSD_CONV - 1):
                shifted = pltpu.roll(ext, SSD_CONV - 1 - k, axis=0)[CONV_HALO:, :]
                acc = acc + cw_ref[k:k + 1, :] * shifted
            halo = y[sub - CONV_HALO:, :]
            o_ref[rows, :] = _silu(acc).astype(o_ref.dtype)
        else:
            o_ref[rows, :] = y.astype(o_ref.dtype)
    if mode == "conv":
        halo_ref[...] = halo


def _proj(h, w, layer, *, mode, n_cols, first_col=0, extra=(), out_dtype, transposed=False, tm=1024, tn=1024):
    s, d = h.shape
    tm = min(tm, s)
    tn = min(tn, n_cols)
    assert n_cols % tn == 0 and first_col % tn == 0
    col0 = first_col // tn
    w_cols = w.shape[1] if transposed else w.shape[2]
    n_valid = min(tn, w_cols - first_col - (n_cols - tn))
    if transposed:
        w_spec = pl.BlockSpec((None, tn, d), lambda j, i: (layer, col0 + j, 0))
    else:
        w_spec = pl.BlockSpec((None, d, tn), lambda j, i: (layer, 0, col0 + j))
    in_specs = [pl.BlockSpec((tm, d), lambda j, i: (i, 0)), w_spec]
    if mode == "residual":
        in_specs.append(pl.BlockSpec((tm, tn), lambda j, i: (i, j)))
    else:
        in_specs += [pl.BlockSpec((e.shape[0], tn), lambda j, i: (0, j)) for e in extra]
    scratch = [pltpu.VMEM((tn, d) if transposed else (d, tn), BF16)]
    if mode == "conv":
        scratch.append(pltpu.VMEM((CONV_HALO, tn), F32))
    return pl.pallas_call(
        functools.partial(_proj_kernel, mode=mode, n_valid=n_valid, transposed=transposed),
        grid=(n_cols // tn, s // tm),
        in_specs=in_specs,
        out_specs=pl.BlockSpec((tm, tn), lambda j, i: (i, j)),
        out_shape=jax.ShapeDtypeStruct((s, n_cols), out_dtype),
        scratch_shapes=scratch,
        compiler_params=_params("arbitrary", "arbitrary"),
        name="proj_" + mode,
    )(h, w, *extra)


def _bias_kernel(tab_ref, o_ref, *, t):
    h = pl.program_id(0)
    sb = LANES
    row = lax.broadcasted_iota(jnp.int32, (sb, sb), 0)
    col = lax.broadcasted_iota(jnp.int32, (sb, sb), 1)
    max_exact = REL_BUCKETS // 2
    far = tab_ref[REL_BUCKETS - 1, h]

    def sub_block(offset):
        dist = row - col + offset
        n = jnp.maximum(dist, 0)
        nf = jnp.maximum(n, 1).astype(F32)
        large = max_exact + (jnp.log(nf / max_exact) / math.log(REL_MAX_DIST / max_exact)
                             * (REL_BUCKETS - max_exact)).astype(jnp.int32)
        large = jnp.minimum(large, REL_BUCKETS - 1)
        bucket = jnp.where(n < max_exact, n, large)
        bias = jnp.zeros((sb, sb), F32)
        for b in range(REL_BUCKETS - 1):
            bias = jnp.where(bucket == b, (tab_ref[b, h] - far) * LOG2E, bias)
        return jnp.where(dist >= 0, bias, NEG_INF)

    for blk in range(2):
        for i in range(t // sb):
            for j in range(t // sb):
                offset = blk * t + (i - j) * sb
                if offset - (sb - 1) >= REL_MAX_DIST:
                    tile = jnp.zeros((sb, sb), F32)
                elif offset + (sb - 1) < 0:
                    tile = jnp.full((sb, sb), NEG_INF, F32)
                else:
                    tile = sub_block(offset)
                o_ref[0, i * sb:(i + 1) * sb, (1 - blk) * t + j * sb:(1 - blk) * t + (j + 1) * sb] = tile


def _bias_tiles(rel_bias, t):
    assert t >= REL_MAX_DIST
    return pl.pallas_call(
        functools.partial(_bias_kernel, t=t),
        grid=(DA_HEADS,),
        in_specs=[pl.BlockSpec(memory_space=pltpu.SMEM)],
        out_specs=pl.BlockSpec((1, t, 2 * t), lambda h: (h, 0, 0)),
        out_shape=jax.ShapeDtypeStruct((DA_HEADS, t, 2 * t), F32),
        compiler_params=_params("arbitrary"),
        name="rel_bias_tiles",
    )(rel_bias)


def _attn_kernel(q_ref, k_ref, v_ref, bias_ref, lv_ref, sg_ref, o_ref, m_ref, l_ref, acc_ref, *, t, lambda_init,
                 online):
    qb = pl.program_id(1)
    dh = DA_HEAD_DIM
    if online:
        m_ref[...] = jnp.full(m_ref.shape, NEG_INF, F32)
    l_ref[...] = jnp.zeros(l_ref.shape, F32)
    acc_ref[...] = jnp.zeros(acc_ref.shape, F32)

    def lane_chunk_sum(p):
        out = p[:, 0:LANES]
        for c in range(1, p.shape[1] // LANES):
            out = out + p[:, c * LANES:(c + 1) * LANES]
        return out

    def block(start, width, bias):
        v = v_ref[pl.ds(start, width), :]
        for mp in range(2):
            cols = slice(mp * dh, (mp + 1) * dh)
            s = _dot_nt(q_ref[:, cols], k_ref[pl.ds(start, width), cols])
            if bias is not None:
                plain = width - bias.shape[1]
                biased = s[:, plain:] + bias
                s = biased if plain == 0 else jnp.concatenate([s[:, :plain], biased], axis=1)
            if online:
                m_prev = m_ref[mp]
                m_new = jnp.maximum(m_prev, jnp.max(s, axis=-1, keepdims=True))
                alpha = jnp.exp2(m_prev - m_new)
                p = jnp.exp2(s - m_new)
                l_ref[mp] = alpha * l_ref[mp] + lane_chunk_sum(p)
                acc_ref[mp] = alpha * acc_ref[mp] + _dot(p.astype(BF16), v)
                m_ref[mp] = m_new
            else:
                p = jnp.exp2(s)
                l_ref[mp] += lane_chunk_sum(p)
                acc_ref[mp] += _dot(p.astype(BF16), v)

    n_far = jnp.maximum(qb - 1, 0)
    n_groups = n_far // FAR_GROUP
    n_left = n_far - n_groups * FAR_GROUP

    def far_group(i, carry):
        block(pl.multiple_of(i * (FAR_GROUP * t), FAR_GROUP * t), FAR_GROUP * t, None)
        return carry

    lax.fori_loop(0, n_groups, far_group, 0)

    for left in range(FAR_GROUP):
        @pl.when(jnp.logical_and(qb > 0, n_left == left))
        def _(left=left):
            block(pl.multiple_of((qb - 1 - left) * t, t), (left + 2) * t, bias_ref[0])

    @pl.when(qb == 0)
    def _():
        block(0, t, bias_ref[0, :, t:2 * t])

    lv = lv_ref[...]
    lam = (jnp.exp(jnp.sum(lv[0:1] * lv[1:2], axis=-1, keepdims=True))
           - jnp.exp(jnp.sum(lv[2:3] * lv[3:4], axis=-1, keepdims=True)) + lambda_init)
    l1 = jnp.sum(l_ref[0], axis=-1, keepdims=True)
    l2 = jnp.sum(l_ref[1], axis=-1, keepdims=True)
    o = acc_ref[0] / l1 - lam * (acc_ref[1] / l2)
    o_ref[...] = (_rms(o, sg_ref[...]) * (1.0 - lambda_init)).astype(o_ref.dtype)


def _diff_attention(qk, v, bias, lam_vecs, subln_gain, lambda_init, logit_bound, *, t):
    s = qk.shape[0]
    t = min(t, s)
    assert s >= (FAR_GROUP + 1) * t
    hw = DA_V_DIM

    def call(online):
        return pl.pallas_call(
            functools.partial(_attn_kernel, t=t, lambda_init=lambda_init, online=online),
            grid=(DA_HEADS, s // t),
            in_specs=[
                pl.BlockSpec((t, hw), lambda h, i: (i, h)),
                pl.BlockSpec((s, hw), lambda h, i: (0, DA_HEADS + h)),
                pl.BlockSpec((s, hw), lambda h, i: (0, h)),
                pl.BlockSpec((1, t, 2 * t), lambda h, i: (h, 0, 0)),
                pl.BlockSpec((4, DA_HEAD_DIM), lambda h, i: (0, 0)),
                pl.BlockSpec((1, hw), lambda h, i: (0, 0)),
            ],
            out_specs=pl.BlockSpec((t, hw), lambda h, i: (i, h)),
            out_shape=jax.ShapeDtypeStruct((s, DA_HEADS * hw), BF16),
            scratch_shapes=[
                pltpu.VMEM((2, t, 1), F32),
                pltpu.VMEM((2, t, LANES), F32),
                pltpu.VMEM((2, t, hw), F32),
            ],
            compiler_params=_params("parallel", "arbitrary"),
            name="diff_attn_online" if online else "diff_attn",
        )

    operands = (qk, qk, v, bias, lam_vecs, subln_gain.reshape(1, hw))
    return lax.cond(logit_bound < MAX_UNSHIFTED_LOG2_LOGIT,
                    lambda *a: call(False)(*a), lambda *a: call(True)(*a), *operands)


def _ssd_kernel(z0_ref, z1_ref, x0_ref, x1_ref, bc_ref, dt_ref, dtb_ref, alog_ref, dskip_ref,
                ng_ref, o_ref, bcs_ref, y_ref, state_ref):
    ln = SSD_CHUNK
    cbw = SSD_COL_BLOCK
    ns = SSD_STATE

    @pl.when(pl.program_id(0) == 0)
    def _():
        state_ref[...] = jnp.zeros(state_ref.shape, F32)

    x_refs = (x0_ref, x1_ref)
    neg_a = -LOG2E * jnp.exp(alog_ref[...])
    row = lax.broadcasted_iota(jnp.int32, (ln, ln), 0)
    col = lax.broadcasted_iota(jnp.int32, (ln, ln), 1)
    causal = row >= col
    lane_lo = lax.broadcasted_iota(jnp.int32, (ln, LANES), 1) < SSD_HEAD_DIM

    for ck in range(z0_ref.shape[0] // ln):
        rows = slice(ck * ln, (ck + 1) * ln)
        bcs_ref[...] = bc_ref[rows, :].astype(BF16)
        dt = jax.nn.softplus(dt_ref[rows, 0:SSD_HEADS] + dtb_ref[...])
        a = dt * neg_a
        acum = jnp.dot(causal.astype(F32), a, preferred_element_type=F32, precision=lax.Precision.HIGHEST)
        acum_t = acum.T

        for g in range(SSD_GROUPS):
            b_g = bcs_ref[:, g * ns:(g + 1) * ns]
            c_g = bcs_ref[:, SSD_GROUPS * ns + g * ns:SSD_GROUPS * ns + (g + 1) * ns]
            cb = _dot_nt(c_g, b_g)
            y_state = _dot(c_g, state_ref[g].astype(BF16))
            xw_parts = []
            decay_parts = []
            for pr in range(SSD_HEADS_PER_GROUP // 2):
                h0 = g * SSD_HEADS_PER_GROUP + 2 * pr
                ch = slice(h0 * SSD_HEAD_DIM, h0 * SSD_HEAD_DIM + LANES)
                m_parts = []
                a_cols = []
                for h in (h0, h0 + 1):
                    a_col = jnp.broadcast_to(acum[:, h:h + 1], (ln, LANES))
                    seg = a_col - acum_t[h:h + 1, :]
                    m_parts.append((cb * jnp.where(causal, jnp.exp2(seg), 0.0)).astype(BF16))
                    a_cols.append(a_col)
                a_pair = jnp.where(lane_lo, a_cols[0], a_cols[1])
                dt_pair = jnp.where(lane_lo, dt[:, h0:h0 + 1], dt[:, h0 + 1:h0 + 2])
                x_pair = x_refs[ch.start // cbw][rows, ch.start % cbw:ch.start % cbw + LANES]
                xdt = x_pair * dt_pair
                xdt_b = xdt.astype(BF16)
                zero = jnp.zeros_like(xdt_b)
                rhs = jnp.concatenate([jnp.where(lane_lo, xdt_b, zero), jnp.where(lane_lo, zero, xdt_b)], axis=0)
                y = _dot(jnp.concatenate(m_parts, axis=1), rhs)
                y = y + y_state[:, pr * LANES:(pr + 1) * LANES] * jnp.exp2(a_pair)
                y_ref[:, ch] = y + x_pair * dskip_ref[:, ch]
                a_last = a_pair[ln - 1:ln, :]
                xw_parts.append((xdt * jnp.exp2(a_last - a_pair)).astype(BF16))
                decay_parts.append(jnp.exp2(a_last))
            xw = jnp.concatenate(xw_parts, axis=1)
            decay = jnp.concatenate(decay_parts, axis=1)
            state_ref[g] = state_ref[g] * decay + _dot_tn(b_g, xw)

        gated = jnp.concatenate([y_ref[:, 0:cbw] * z0_ref[rows, :], y_ref[:, cbw:2 * cbw] * z1_ref[rows, :]], axis=1)
        o_ref[rows, :] = _rms(gated, ng_ref[...]).astype(o_ref.dtype)


def _ssd_scan(z_act, xbc, dt_raw, dt_bias, a_log, d_skip, norm_g):
    s = z_act.shape[0]
    ln = SSD_CHUNK
    rows = min(SSD_CHUNKS_PER_STEP * ln, s)
    cbw = SSD_COL_BLOCK
    assert SSD_INNER == 2 * cbw and 2 * SSD_GROUPS * SSD_STATE == cbw

    def col_block(j):
        return pl.BlockSpec((rows, cbw), lambda c: (c, j))

    def whole(shape):
        return pl.BlockSpec(shape, lambda c: (0,) * len(shape))

    return pl.pallas_call(
        _ssd_kernel,
        grid=(s // rows,),
        in_specs=[
            col_block(0), col_block(1), col_block(0), col_block(1), col_block(2),
            pl.BlockSpec((rows, LANES), lambda c: (c, 0)),
            whole((1, SSD_HEADS)), whole((1, SSD_HEADS)),
            whole((1, SSD_INNER)), whole((1, SSD_INNER)),
        ],
        out_specs=pl.BlockSpec((rows, SSD_INNER), lambda c: (c, 0)),
        out_shape=jax.ShapeDtypeStruct((s, SSD_INNER), BF16),
        scratch_shapes=[
            pltpu.VMEM((ln, 2 * SSD_GROUPS * SSD_STATE), BF16),
            pltpu.VMEM((ln, SSD_INNER), F32),
            pltpu.VMEM((SSD_GROUPS, SSD_STATE, SSD_GROUP_WIDTH), F32),
        ],
        compiler_params=_params("arbitrary"),
        name="ssd_scan",
    )(z_act, z_act, xbc, xbc, xbc, dt_raw,
      dt_bias.reshape(1, SSD_HEADS), a_log.reshape(1, SSD_HEADS),
      jnp.repeat(d_skip, SSD_HEAD_DIM).reshape(1, SSD_INNER), norm_g.reshape(1, SSD_INNER))


def _head_rms(y, gain, n_heads):
    return [_rms(y[:, h * XA_HEAD_DIM:(h + 1) * XA_HEAD_DIM], gain) for h in range(n_heads)]


def _mem_kv_kernel(mem_ref, g_ref, w_ref, kg_ref, k_ref, v_ref):
    xw = XA_HEADS * XA_HEAD_DIM
    kv = _dot(_rms(mem_ref[...], g_ref[...]).astype(BF16), w_ref[...].astype(BF16))
    k_ref[...] = jnp.concatenate(_head_rms(kv[:, 0:xw], kg_ref[...], XA_HEADS), axis=1).astype(BF16)
    v_ref[...] = kv[:, xw:2 * xw].astype(BF16)


def _mem_kv(mem, g, w_kv, layer, k_gain):
    m, d = mem.shape
    xw = XA_HEADS * XA_HEAD_DIM

    def whole(shape):
        return pl.BlockSpec(shape, lambda i: (0,) * len(shape))

    return pl.pallas_call(
        _mem_kv_kernel,
        grid=(1,),
        in_specs=[whole((m, d)), whole((1, d)), pl.BlockSpec((None, d, 2 * xw), lambda i: (layer, 0, 0)),
                  whole((1, XA_HEAD_DIM))],
        out_specs=(whole((m, xw)), whole((m, xw))),
        out_shape=(jax.ShapeDtypeStruct((m, xw), BF16), jax.ShapeDtypeStruct((m, xw), BF16)),
        compiler_params=_params("arbitrary"),
        name="mem_kv",
    )(mem, g.reshape(1, d), w_kv, k_gain.reshape(1, XA_HEAD_DIM))


def _xattn_kernel(x_ref, g_ref, wq_ref, qg_ref, k_ref, v_ref, wo_ref, o_ref, wq_b_ref, wo_b_ref):
    @pl.when(pl.program_id(0) == 0)
    def _():
        wq_b_ref[...] = wq_ref[...].astype(BF16)
        wo_b_ref[...] = wo_ref[...].astype(BF16)

    x = x_ref[...]
    q = _dot(_rms(x, g_ref[...]).astype(BF16), wq_b_ref[...])
    heads = []
    for h, qh in enumerate(_head_rms(q, qg_ref[...], XA_HEADS)):
        cols = slice(h * XA_HEAD_DIM, (h + 1) * XA_HEAD_DIM)
        s = _dot_nt((qh * XA_HEAD_DIM ** -0.5).astype(BF16), k_ref[:, cols])
        p = jnp.exp(s - jnp.max(s, axis=-1, keepdims=True))
        p = p / jnp.sum(p, axis=-1, keepdims=True)
        heads.append(_dot(p.astype(BF16), v_ref[:, cols]))
    o = jnp.concatenate(heads, axis=1).astype(BF16)
    o_ref[...] = x + _dot(o, wo_b_ref[...])


def _xattn(x, g, w_q, q_gain, k, v, w_o, layer, *, tm=512):
    s, d = x.shape
    m, xw = k.shape
    tm = min(tm, s)

    def whole(shape):
        return pl.BlockSpec(shape, lambda i: (0,) * len(shape))

    return pl.pallas_call(
        _xattn_kernel,
        grid=(s // tm,),
        in_specs=[
            pl.BlockSpec((tm, d), lambda i: (i, 0)),
            whole((1, d)), pl.BlockSpec((None, d, xw), lambda i: (layer, 0, 0)), whole((1, XA_HEAD_DIM)),
            whole((m, xw)), whole((m, xw)), pl.BlockSpec((None, xw, d), lambda i: (layer, 0, 0)),
        ],
        out_specs=pl.BlockSpec((tm, d), lambda i: (i, 0)),
        out_shape=jax.ShapeDtypeStruct((s, d), F32),
        scratch_shapes=[pltpu.VMEM((d, xw), BF16), pltpu.VMEM((xw, d), BF16)],
        compiler_params=_params("arbitrary"),
        name="mem_xattn",
    )(x, g.reshape(1, d), w_q, q_gain.reshape(1, XA_HEAD_DIM), k, v, w_o)


def _attn_layer(x, h, w_qkv, w_o, layer, q_gain, k_gain, lam_vecs, subln_gain, rel_bias, bias, lambda_init, t):
    d = x.shape[1]
    n_maps = 2 * DA_HEADS
    q_col_gain = q_gain * (DA_HEAD_DIM ** -0.5 * LOG2E)
    col_gain = jnp.concatenate([jnp.tile(q_col_gain, n_maps), jnp.tile(k_gain, n_maps)]).reshape(1, 2 * d)
    qk = _proj(h, w_qkv, layer, mode="qk_norm", n_cols=2 * d, extra=(col_gain,), out_dtype=BF16)
    v = _proj(h, w_qkv, layer, mode="plain", n_cols=d, first_col=2 * d, out_dtype=BF16)
    logit_bound = (DA_HEAD_DIM * jnp.max(jnp.abs(q_col_gain)) * jnp.max(jnp.abs(k_gain))
                   + LOG2E * jnp.max(jnp.abs(rel_bias - rel_bias[REL_BUCKETS - 1])))
    o = _diff_attention(qk, v, bias, lam_vecs, subln_gain, lambda_init, logit_bound, t=t)
    return _proj(o, w_o, layer, mode="residual", n_cols=d, extra=(x,), out_dtype=F32)


def _ssd_layer(x, h, w_in, w_out, layer, conv_w, conv_b, dt_bias, a_log, d_skip, ssd_norm):
    n_main = 2 * SSD_INNER + 2 * SSD_GROUPS * SSD_STATE
    w_in_t = jnp.swapaxes(w_in, 1, 2)
    z_act = _proj(h, w_in_t, layer, mode="silu", n_cols=SSD_INNER, out_dtype=F32, transposed=True)
    xbc = _proj(h, w_in_t, layer, mode="conv", n_cols=n_main - SSD_INNER, first_col=SSD_INNER,
                extra=(conv_w, conv_b.reshape(1, -1)), out_dtype=F32, transposed=True)
    dt_raw = _proj(h, w_in_t, layer, mode="plain", n_cols=LANES, first_col=n_main, out_dtype=F32, tn=LANES,
                   transposed=True)
    y = _ssd_scan(z_act, xbc, dt_raw, dt_bias, a_log, d_skip, ssd_norm)
    return _proj(y, w_out, layer, mode="residual", n_cols=x.shape[1], extra=(x,), out_dtype=F32, tn=512)


@jax.jit
def kernel(x, mem, rel_bias, ffn1_norm, ffn1_w_gate, ffn1_w_up, ffn1_w_down, mix_norm, attn_w_qkv, attn_w_o,
           attn_q_norm, attn_k_norm, attn_lambda, attn_subln, ssd_w_in, ssd_conv_w, ssd_conv_b, ssd_dt_bias,
           ssd_a_log, ssd_d, ssd_norm, ssd_w_out, xattn_norm, mem_norm, xattn_w_q, xattn_w_kv, xattn_w_o,
           xattn_q_norm, xattn_k_norm, ffn2_norm, ffn2_w_gate, ffn2_w_up, ffn2_w_down):
    b, s, d = x.shape
    assert b == 1
    t = min(ATTN_BLOCK, s)
    xs = x[0]
    mem2 = mem[0]
    bias = _bias_tiles(rel_bias, t)
    for i in range(DEPTH):
        xs, h_mix = _ffn(xs, ffn1_norm[i], ffn1_w_gate, ffn1_w_up, ffn1_w_down, i, next_gain=mix_norm[i])
        j = i // N_MIXERS
        if i % N_MIXERS == 0:
            lambda_init = 0.8 - 0.6 * math.exp(-0.3 * i)
            xs = _attn_layer(xs, h_mix, attn_w_qkv, attn_w_o, j, attn_q_norm[j], attn_k_norm[j],
                             attn_lambda[j], attn_subln[j], rel_bias, bias, lambda_init, t)
        else:
            xs = _ssd_layer(xs, h_mix, ssd_w_in, ssd_w_out, j, ssd_conv_w[j], ssd_conv_b[j], ssd_dt_bias[j],
                            ssd_a_log[j], ssd_d[j], ssd_norm[j])
        k, v = _mem_kv(mem2, mem_norm[i], xattn_w_kv, i, xattn_k_norm[i])
        xs = _xattn(xs, xattn_norm[i], xattn_w_q, xattn_q_norm[i], k, v, xattn_w_o, i)
        xs = _ffn(xs, ffn2_norm[i], ffn2_w_gate, ffn2_w_up, ffn2_w_down, i)
    return xs[None]
```
